```python
import math
import jax, jax.numpy as jnp
from jax import lax
import numpy as np

D_MODEL = 2048
BATCH = 4
SEQ = 8192
DEPTH = 2

N_MIXERS = 2
N_S5 = (DEPTH + 1) // 2
N_CONV = DEPTH // 2
S5_GROUP = 16
S5_GROUPS = D_MODEL // S5_GROUP
S5_STATE = 64
S5_CHUNK = 128
DT_MIN = 1e-3
DT_MAX = 1e-1
CONV_WIDTH = 31
D_FF = ((8 * D_MODEL + 3 * 256 - 1) // (3 * 256)) * 256
ALPHA = (2 * DEPTH) ** 0.25
BETA = (8 * DEPTH) ** -0.25
LN_EPS = 1e-5
N_MOD = 6

kernel_name = 'hybrid_s5_conformer_conv_deepnorm_adaln'


def layer_norm(x, g, b):
    xf = x.astype(jnp.float32)
    mu = jnp.mean(xf, axis=-1, keepdims=True)
    xc = xf - mu
    var = jnp.mean(xc * xc, axis=-1, keepdims=True)
    return (xc * lax.rsqrt(var + LN_EPS) * g.astype(jnp.float32) + b.astype(jnp.float32)).astype(x.dtype)


def _ssm_combine(e1, e2):
    a1, b1 = e1
    a2, b2 = e2
    return a1 * a2, a2 * b1 + b2


def s5_mixer(u, lam_re, lam_im, log_dt, b_re, b_im, c_re, c_im, d_skip, w_glu):
    f32 = jnp.float32
    bsz, seq, d = u.shape
    lam = lax.complex(lam_re.astype(f32), lam_im.astype(f32))
    dt = jnp.exp(log_dt.astype(f32))[:, None]
    a_bar = jnp.exp(lam * dt)
    b_mat = lax.complex(b_re.astype(f32), b_im.astype(f32))
    b_bar = ((a_bar - 1.0) / lam)[..., None] * b_mat
    c_mat = lax.complex(c_re.astype(f32), c_im.astype(f32))
    steps = jnp.arange(1, S5_CHUNK + 1, dtype=f32)[:, None, None]
    a_pow = jnp.exp(lam * dt * steps)
    a_elems = jnp.broadcast_to(a_bar, (bsz, S5_CHUNK, S5_GROUPS, S5_STATE))

    n_chunks = seq // S5_CHUNK
    u_chunks = u.astype(f32).reshape(bsz, n_chunks, S5_CHUNK, S5_GROUPS, S5_GROUP)
    u_chunks = jnp.transpose(u_chunks, (1, 0, 2, 3, 4))

    def chunk_step(h0, u_c):
        bu = jnp.einsum('btgi,gpi->btgp', u_c.astype(jnp.complex64), b_bar)
        _, h_local = lax.associative_scan(_ssm_combine, (a_elems, bu), axis=1)
        h = h_local + a_pow[None] * h0[:, None]
        y_c = jnp.real(jnp.einsum('btgp,gip->btgi', h, c_mat))
        return h[:, -1], y_c

    h_init = jnp.zeros((bsz, S5_GROUPS, S5_STATE), jnp.complex64)
    _, y = lax.scan(chunk_step, h_init, u_chunks)
    y = jnp.transpose(y, (1, 0, 2, 3, 4)).reshape(bsz, seq, d)
    y = jax.nn.gelu(y + d_skip.astype(f32) * u.astype(f32)).astype(u.dtype)
    vg = y @ w_glu
    return vg[..., :d] * jax.nn.sigmoid(vg[..., d:])


def conv_mixer(u, w_pw1, w_dw, b_dw, g_norm, b_norm, w_pw2):
    d = u.shape[-1]
    ab = u @ w_pw1
    v = ab[..., :d] * jax.nn.sigmoid(ab[..., d:])
    v = lax.conv_general_dilated(
        v, w_dw[:, None, :], window_strides=(1,),
        padding=((CONV_WIDTH - 1, 0),),
        dimension_numbers=('NWC', 'WIO', 'NWC'),
        feature_group_count=d) + b_dw
    v = layer_norm(v, g_norm, b_norm)
    return jax.nn.silu(v) @ w_pw2


def swiglu_ffn(u, w_gu, w_down):
    gu = u @ w_gu
    g, v = jnp.split(gu, 2, axis=-1)
    return (jax.nn.silu(g) * v) @ w_down


def setup_inputs(seed: int = 0) -> dict:
    key = jax.random.key(seed)
    ks = jax.random.split(key, 24)
    f32 = jnp.float32
    D, G, P, I, F, W = D_MODEL, S5_GROUPS, S5_STATE, S5_GROUP, D_FF, CONV_WIDTH
    nrm = lambda k, shape, s: jax.random.normal(k, shape, f32) * s
    n_idx = jnp.arange(P, dtype=f32)
    return {
        'x': nrm(ks[0], (BATCH, SEQ, D), 1.0),
        'c': nrm(ks[1], (BATCH, D), 1.0),
        'ada_w': nrm(ks[2], (DEPTH, D, N_MOD * D), 0.1 * D ** -0.5),
        'ada_b': nrm(ks[3], (DEPTH, N_MOD * D), 0.01),
        'ln_g': 1.0 + nrm(ks[4], (DEPTH, 2, D), 0.01),
        'ln_b': nrm(ks[5], (DEPTH, 2, D), 0.01),
        's5_lam_re': -0.5 + nrm(ks[6], (N_S5, G, P), 0.01),
        's5_lam_im': math.pi * n_idx + nrm(ks[7], (N_S5, G, P), 0.01),
        's5_log_dt': jax.random.uniform(ks[8], (N_S5, G), f32, math.log(DT_MIN), math.log(DT_MAX)),
        's5_b_re': nrm(ks[9], (N_S5, G, P, I), (2 * I) ** -0.5),
        's5_b_im': nrm(ks[10], (N_S5, G, P, I), (2 * I) ** -0.5),
        's5_c_re': nrm(ks[11], (N_S5, G, I, P), (2 * P) ** -0.5),
        's5_c_im': nrm(ks[12], (N_S5, G, I, P), (2 * P) ** -0.5),
        's5_d': nrm(ks[13], (N_S5, D), 1.0),
        's5_w_glu': nrm(ks[14], (N_S5, D, 2 * D), BETA * D ** -0.5),
        'cv_w_pw1': nrm(ks[15], (N_CONV, D, 2 * D), D ** -0.5),
        'cv_w_dw': nrm(ks[16], (N_CONV, W, D), W ** -0.5),
        'cv_b_dw': nrm(ks[17], (N_CONV, D), 0.01),
        'cv_norm_g': 1.0 + nrm(ks[18], (N_CONV, D), 0.01),
        'cv_norm_b': nrm(ks[19], (N_CONV, D), 0.01),
        'cv_w_pw2': nrm(ks[20], (N_CONV, D, D), BETA * D ** -0.5),
        'ffn_w_gu': nrm(ks[21], (DEPTH, D, 2 * F), D ** -0.5),
        'ffn_w_down': nrm(ks[22], (DEPTH, F, D), BETA * F ** -0.5),
    }


def reference(x, c, ada_w, ada_b, ln_g, ln_b, s5_lam_re, s5_lam_im, s5_log_dt, s5_b_re, s5_b_im,
              s5_c_re, s5_c_im, s5_d, s5_w_glu, cv_w_pw1, cv_w_dw, cv_b_dw, cv_norm_g, cv_norm_b,
              cv_w_pw2, ffn_w_gu, ffn_w_down):
    c_act = jax.nn.silu(c)
    for i in range(DEPTH):
        mod = c_act @ ada_w[i] + ada_b[i]
        sh_m, sc_m, gt_m, sh_f, sc_f, gt_f = [m[:, None, :] for m in jnp.split(mod, N_MOD, axis=-1)]
        j = i // N_MIXERS
        h = x * (1.0 + sc_m) + sh_m
        if i % N_MIXERS == 0:
            y = s5_mixer(h, s5_lam_re[j], s5_lam_im[j], s5_log_dt[j], s5_b_re[j], s5_b_im[j],
                         s5_c_re[j], s5_c_im[j], s5_d[j], s5_w_glu[j])
        else:
            y = conv_mixer(h, cv_w_pw1[j], cv_w_dw[j], cv_b_dw[j], cv_norm_g[j], cv_norm_b[j], cv_w_pw2[j])
        x = layer_norm(ALPHA * x + (1.0 + gt_m) * y, ln_g[i, 0], ln_b[i, 0])
        h = x * (1.0 + sc_f) + sh_f
        y = swiglu_ffn(h, ffn_w_gu[i], ffn_w_down[i])
        x = layer_norm(ALPHA * x + (1.0 + gt_f) * y, ln_g[i, 1], ln_b[i, 1])
    return x
```

```python
import functools
import math

import jax
import jax.numpy as jnp
from jax import lax
from jax.experimental import pallas as pl
from jax.experimental.pallas import tpu as pltpu

F32 = jnp.float32
BF16 = jnp.bfloat16

D_MODEL = 2048
BATCH = 4
SEQ = 8192
DEPTH = 2
S5_GROUP = 16
S5_GROUPS = D_MODEL // S5_GROUP
S5_STATE = 64
CONV_WIDTH = 31
D_FF = 5632
ALPHA = (2 * DEPTH) ** 0.25
LN_EPS = 1e-5
N_MOD = 6

LANES = 128
SUBLANES = 8
MXU_DIM = 256

TAU = 8
ROWS = SEQ // TAU
GPB = LANES // S5_GROUP
NBLK = D_MODEL // LANES
NPAIR = D_MODEL // MXU_DIM
KDIM = TAU * LANES
SDIM = GPB * S5_STATE * 2
S5_RT = 512
NSEG = SUBLANES
SEG = 66
S5_RP = NSEG * SEG
NSLAB = SDIM // LANES

VMEM_LIMIT = 56 * 1024 * 1024


def _layer_norm(z, g, b):
    mu = jnp.mean(z, axis=-1, keepdims=True)
    zc = z - mu
    var = jnp.mean(zc * zc, axis=-1, keepdims=True)
    return zc * lax.rsqrt(var + LN_EPS) * g + b


def _gelu_tanh(x):
    c = math.sqrt(2.0 / math.pi)
    return x * (0.5 * (1.0 + jnp.tanh(c * (x + 0.044715 * (x * x * x)))))


def _bdot(a, b):
    return jnp.dot(a, b, preferred_element_type=F32)


ADA_TN = 1024


def _ada_kernel(c_ref, w_ref, b_ref, o_ref):
    c = c_ref[...]
    ca = c * jax.nn.sigmoid(c)
    o_ref[0] = jnp.dot(ca, w_ref[0], preferred_element_type=F32,
                       precision=lax.Precision.HIGHEST) + b_ref[0]


def _ada(c, ada_w, ada_b):
    cp = jnp.zeros((SUBLANES, D_MODEL), F32).at[:BATCH].set(c)
    n = N_MOD * D_MODEL
    out = pl.pallas_call(
        _ada_kernel,
        grid=(DEPTH, n // ADA_TN),
        in_specs=[
            pl.BlockSpec((SUBLANES, D_MODEL), lambda i, j: (0, 0)),
            pl.BlockSpec((1, D_MODEL, ADA_TN), lambda i, j: (i, 0, j)),
            pl.BlockSpec((1, 1, ADA_TN), lambda i, j: (i, 0, j)),
        ],
        out_specs=pl.BlockSpec((1, SUBLANES, ADA_TN), lambda i, j: (i, 0, j)),
        out_shape=jax.ShapeDtypeStruct((DEPTH, SUBLANES, n), F32),
        compiler_params=pltpu.CompilerParams(
            dimension_semantics=("parallel", "parallel"), vmem_limit_bytes=VMEM_LIMIT),
        name="ada",
    )(cp, ada_w, ada_b.reshape(DEPTH, 1, n))
    return out[:, :BATCH]


def _cmul(a, b):
    return a[0] * b[0] - a[1] * b[1], a[0] * b[1] + a[1] * b[0]


def _s5_prep(lam_re, lam_im, log_dt, b_re, b_im, c_re, c_im):
    hp = lax.Precision.HIGHEST
    dt = jnp.exp(log_dt)[:, None]
    zr, zi = lam_re * dt, lam_im * dt
    er = jnp.exp(zr)
    a = (er * jnp.cos(zi), er * jnp.sin(zi))
    sh = jnp.sin(0.5 * zi)
    nr, ni = jnp.expm1(zr) * jnp.cos(zi) - 2.0 * sh * sh, a[1]
    den = lam_re * lam_re + lam_im * lam_im
    q = ((nr * lam_re + ni * lam_im) / den, (ni * lam_re - nr * lam_im) / den)
    bb = (q[0][..., None] * b_re - q[1][..., None] * b_im,
          q[0][..., None] * b_im + q[1][..., None] * b_re)
    pows = [(jnp.ones_like(zr), jnp.zeros_like(zr))]
    for _ in range(TAU):
        pows.append(_cmul(pows[-1], a))
    pr = jnp.stack([p[0] for p in pows])
    pi = jnp.stack([p[1] for p in pows])
    car = c_re[None] * pr[:, :, None, :] - c_im[None] * pi[:, :, None, :]
    cai = c_re[None] * pi[:, :, None, :] + c_im[None] * pr[:, :, None, :]
    kk = (jnp.einsum("kgop,gpi->kgoi", car[:TAU], bb[0], precision=hp)
          - jnp.einsum("kgop,gpi->kgoi", cai[:TAU], bb[1], precision=hp))
    eye = jnp.eye(GPB, dtype=F32)
    g2 = (NBLK, GPB)

    jj = jnp.arange(TAU)
    lag = jj[None, :] - jj[:, None]
    kf = jnp.where((lag >= 0)[:, :, None, None, None], kk[jnp.clip(lag, 0, TAU - 1)], 0.0)
    kf = kf.reshape(TAU, TAU, *g2, S5_GROUP, S5_GROUP)
    kf = jnp.transpose(kf, (2, 0, 3, 5, 1, 4))
    w_toep = kf[:, :, :, :, :, None, :] * eye[None, None, :, None, None, :, None]
    w_toep = w_toep.reshape(NBLK, KDIM, KDIM)

    vr = pr[TAU - 1::-1][:, :, :, None] * bb[0][None] - pi[TAU - 1::-1][:, :, :, None] * bb[1][None]
    vi = pr[TAU - 1::-1][:, :, :, None] * bb[1][None] + pi[TAU - 1::-1][:, :, :, None] * bb[0][None]
    v = jnp.stack([vr, vi], axis=2)
    v = v.reshape(TAU, *g2, 2, S5_STATE, S5_GROUP)
    v = jnp.transpose(v, (1, 0, 2, 5, 3, 4))
    w_in = v[:, :, :, :, :, None, :] * eye[None, None, :, None, None, :, None]
    w_in = w_in.reshape(NBLK, KDIM, SDIM)

    co = jnp.stack([car[1:], -cai[1:]], axis=0)
    co = co.reshape(2, TAU, *g2, S5_GROUP, S5_STATE)
    co = jnp.transpose(co, (2, 0, 3, 5, 1, 4))
    w_out = co[:, :, :, :, :, None, :] * eye[None, None, :, None, None, :, None]
    w_out = w_out.reshape(NBLK, SDIM, KDIM)

    a_t = pows[TAU]
    sq = [a_t]
    for _ in range(6):
        sq.append(_cmul(sq[-1], sq[-1]))
    assert SEG == 66
    a_seg = _cmul(sq[6], sq[1])
    coef = jnp.stack([a_t[0], a_t[1], a_seg[0], a_seg[1]], axis=0)
    coef = jnp.transpose(coef.reshape(4, NBLK, GPB * S5_STATE), (1, 0, 2))

    w_ui = jnp.concatenate([w_toep, w_in], axis=2).astype(BF16)
    return w_ui, w_out.astype(BF16), coef


def _s5_kernel(*refs):
    x_refs = refs[:TAU]
    (sc_ref, sh_ref, d_ref, wui_ref, wout_ref, coef_ref, o_ref,
     x_scr, hp_scr, carry_scr, s_scr) = refs[TAU:]
    half = NSLAB // 2

    @pl.when(pl.program_id(2) == 0)
    def _():
        carry_scr[...] = jnp.zeros_like(carry_scr)

    for hf in range(2):
        ls = hf * LANES
        scale = 1.0 + sc_ref[0, :, ls:ls + LANES]
        shift = sh_ref[0, :, ls:ls + LANES]

        def h_of(j):
            return x_refs[j][0, :, ls:ls + LANES] * scale + shift

        u = jnp.concatenate([h_of(j).astype(BF16) for j in range(TAU)], axis=1)
        r = _bdot(u, wui_ref[hf])
        for l in range(NSLAB):
            x_scr[l, 0:S5_RT, :] = r[:, KDIM + l * LANES:KDIM + (l + 1) * LANES]
            x_scr[l, S5_RT:S5_RP, :] = jnp.zeros((S5_RP - S5_RT, LANES), F32)

        def coef(row, l):
            return jnp.broadcast_to(coef_ref[hf, row:row + 1, l * LANES:(l + 1) * LANES],
                                    (NSEG, LANES))

        ar = [coef(0, l) for l in range(half)]
        ai = [coef(1, l) for l in range(half)]

        def step(k, hr, hi):
            nr, ni = [], []
            for l in range(half):
                xr = x_scr[l, pl.ds(k, NSEG, stride=SEG), :]
                xi = x_scr[l + half, pl.ds(k, NSEG, stride=SEG), :]
                nr.append(ar[l] * hr[l] - ai[l] * hi[l] + xr)
                ni.append(ar[l] * hi[l] + ai[l] * hr[l] + xi)
            return tuple(nr), tuple(ni)

        zero = tuple(jnp.zeros((NSEG, LANES), F32) for _ in range(half))
        er, ei = lax.fori_loop(0, SEG, lambda k, c: step(k, *c), (zero, zero))

        cr = [carry_scr[hf * NSLAB + l, 0:1, :] for l in range(half)]
        ci = [carry_scr[hf * NSLAB + half + l, 0:1, :] for l in range(half)]
        for s in range(NSEG):
            for l in range(half):
                s_scr[l, s:s + 1, :] = cr[l]
                s_scr[l + half, s:s + 1, :] = ci[l]
            if s + 1 < NSEG:
                for l in range(half):
                    gr = coef_ref[hf, 2:3, l * LANES:(l + 1) * LANES]
                    gi = coef_ref[hf, 3:4, l * LANES:(l + 1) * LANES]
                    nr = gr * cr[l] - gi * ci[l] + er[l][s:s + 1, :]
                    ni = gr * ci[l] + gi * cr[l] + ei[l][s:s + 1, :]
                    cr[l], ci[l] = nr, ni

        def step2(k, c):
            hr, hi = c
            for l in range(half):
                hp_scr[l, pl.ds(k, NSEG, stride=SEG), :] = hr[l]
                hp_scr[l + half, pl.ds(k, NSEG, stride=SEG), :] = hi[l]
            return step(k, hr, hi)

        start = (tuple(s_scr[l] for l in range(half)),
                 tuple(s_scr[l + half] for l in range(half)))
        lax.fori_loop(0, SEG, step2, start)
        for l in range(NSLAB):
            carry_scr[hf * NSLAB + l, 0:1, :] = hp_scr[l, S5_RT:S5_RT + 1, :]

        hp = jnp.concatenate([hp_scr[l, 0:S5_RT, :].astype(BF16) for l in range(NSLAB)], axis=1)
        y = r[:, :KDIM] + _bdot(hp, wout_ref[hf])
        dsk = d_ref[:, ls:ls + LANES]
        for j in range(TAU):
            yj = y[:, j * LANES:(j + 1) * LANES] + dsk * h_of(j)
            o_ref[0, 0, :, j * MXU_DIM + ls:j * MXU_DIM + ls + LANES] = _gelu_tanh(yj).astype(BF16)


def _s5(xr, sc, sh, d_skip, w_ui, w_out, coef):
    nper = D_MODEL // MXU_DIM
    x_specs = [
        pl.BlockSpec((1, S5_RT, MXU_DIM), functools.partial(
            lambda bp, b, rt, j: (b, rt, j * nper + bp), j=j))
        for j in range(TAU)
    ]
    mod_spec = pl.BlockSpec((1, 1, MXU_DIM), lambda bp, b, rt: (b, 0, bp))
    return pl.pallas_call(
        _s5_kernel,
        grid=(NPAIR, BATCH, ROWS // S5_RT),
        in_specs=x_specs + [
            mod_spec, mod_spec,
            pl.BlockSpec((1, MXU_DIM), lambda bp, b, rt: (0, bp)),
            pl.BlockSpec((2, KDIM, KDIM + SDIM), lambda bp, b, rt: (bp, 0, 0),
                         pipeline_mode=pl.Buffered(1)),
            pl.BlockSpec((2, SDIM, KDIM), lambda bp, b, rt: (bp, 0, 0),
                         pipeline_mode=pl.Buffered(1)),
            pl.BlockSpec((2, 4, SDIM // 2), lambda bp, b, rt: (bp, 0, 0)),
        ],
        out_specs=pl.BlockSpec((1, 1, S5_RT, TAU * MXU_DIM), lambda bp, b, rt: (b, bp, rt, 0)),
        out_shape=jax.ShapeDtypeStruct((BATCH, NPAIR, ROWS, TAU * MXU_DIM), BF16),
        scratch_shapes=[
            pltpu.VMEM((NSLAB, S5_RP, LANES), F32),
            pltpu.VMEM((NSLAB, S5_RP, LANES), F32),
            pltpu.VMEM((2 * NSLAB, SUBLANES, LANES), F32),
            pltpu.VMEM((NSLAB, NSEG, LANES), F32),
        ],
        compiler_params=pltpu.CompilerParams(
            dimension_semantics=("parallel", "parallel", "arbitrary"),
            vmem_limit_bytes=VMEM_LIMIT),
        name="s5",
    )(*([xr] * TAU), sc, sh, d_skip.reshape(1, D_MODEL), w_ui, w_out, coef)


GLU_RT = 512
COL_T = 512


def _glu_ln_kernel(y_ref, x_ref, gt_ref, w_ref, g_ref, b_ref, o_ref):
    y = jnp.concatenate([y_ref[0, bp] for bp in range(NPAIR)], axis=1)
    zs = []
    for ct in range(D_MODEL // COL_T):
        c0 = ct * COL_T
        v = _bdot(y, w_ref[:, c0:c0 + COL_T])
        g = _bdot(y, w_ref[:, D_MODEL + c0:D_MODEL + c0 + COL_T])
        gate = 1.0 + gt_ref[0, :, c0:c0 + COL_T]
        zs.append(ALPHA * x_ref[0, :, c0:c0 + COL_T] + gate * (v * jax.nn.sigmoid(g)))
    z = jnp.concatenate(zs, axis=1)
    o_ref[0] = _layer_norm(z, g_ref[...], b_ref[...])


def _glu_ln(ycat, xr, gt, w_glu, g, b):
    vec = pl.BlockSpec((1, D_MODEL), lambda bb, rt, j: (0, 0))
    return pl.pallas_call(
        _glu_ln_kernel,
        grid=(BATCH, ROWS // GLU_RT, TAU),
        in_specs=[
            pl.BlockSpec((1, NPAIR, GLU_RT, MXU_DIM), lambda bb, rt, j: (bb, 0, rt, j)),
            pl.BlockSpec((1, GLU_RT, D_MODEL), lambda bb, rt, j: (bb, rt, j)),
            pl.BlockSpec((1, 1, D_MODEL), lambda bb, rt, j: (bb, 0, 0)),
            pl.BlockSpec((D_MODEL, 2 * D_MODEL), lambda bb, rt, j: (0, 0),
                         pipeline_mode=pl.Buffered(1)),
            vec, vec,
        ],
        out_specs=pl.BlockSpec((1, GLU_RT, D_MODEL), lambda bb, rt, j: (bb, rt, j)),
        out_shape=jax.ShapeDtypeStruct((BATCH, ROWS, TAU * D_MODEL), F32),
        compiler_params=pltpu.CompilerParams(
            dimension_semantics=("parallel", "parallel", "parallel"),
            vmem_limit_bytes=VMEM_LIMIT),
        name="glu_ln",
    )(ycat, xr, gt, w_glu, g.reshape(1, D_MODEL), b.reshape(1, D_MODEL))


FFN_TM = 512
FFN_TF = 512


def _ffn_kernel(x_ref, sc_ref, sh_ref, gt_ref, wg_ref, wv_ref, wd_ref, g_ref, b_ref, o_ref,
                h_scr, acc_scr):
    f = pl.program_id(1)

    @pl.when(f == 0)
    def _():
        h_scr[...] = (x_ref[...] * (1.0 + sc_ref[0]) + sh_ref[0]).astype(BF16)
        acc_scr[...] = jnp.zeros_like(acc_scr)

    h = h_scr[...]
    g = _bdot(h, wg_ref[...])
    v = _bdot(h, wv_ref[...])
    a = (g * jax.nn.sigmoid(g) * v).astype(BF16)
    acc_scr[...] += _bdot(a, wd_ref[...])

    @pl.when(f == pl.num_programs(1) - 1)
    def _():
        z = ALPHA * x_ref[...] + (1.0 + gt_ref[0]) * acc_scr[...]
        o_ref[...] = _layer_norm(z, g_ref[...], b_ref[...])


def _ffn(x2, sc, sh, gt, w_gu, w_down, g, b):
    m = x2.shape[0]
    tpb = SEQ // FFN_TM
    nf = D_FF // FFN_TF
    mod = pl.BlockSpec((1, 1, D_MODEL), lambda i, f: (i // tpb, 0, 0))
    vec = pl.BlockSpec((1, D_MODEL), lambda i, f: (0, 0))
    return pl.pallas_call(
        _ffn_kernel,
        grid=(m // FFN_TM, nf),
        in_specs=[
            pl.BlockSpec((FFN_TM, D_MODEL), lambda i, f: (i, 0)),
            mod, mod, mod,
            pl.BlockSpec((D_MODEL, FFN_TF), lambda i, f: (0, f)),
            pl.BlockSpec((D_MODEL, FFN_TF), lambda i, f: (0, nf + f)),
            pl.BlockSpec((FFN_TF, D_MODEL), lambda i, f: (f, 0)),
            vec, vec,
        ],
        out_specs=pl.BlockSpec((FFN_TM, D_MODEL), lambda i, f: (i, 0)),
        out_shape=jax.ShapeDtypeStruct((m, D_MODEL), F32),
        scratch_shapes=[pltpu.VMEM((FFN_TM, D_MODEL), BF16), pltpu.VMEM((FFN_TM, D_MODEL), F32)],
        compiler_params=pltpu.CompilerParams(
            dimension_semantics=("parallel", "arbitrary"), vmem_limit_bytes=VMEM_LIMIT),
        name="ffn",
    )(x2, sc, sh, gt, w_gu, w_gu, w_down, g.reshape(1, D_MODEL), b.reshape(1, D_MODEL))


PW_TM = 512
CV_TM = 256
HALO = 32


def _pw1_kernel(x_ref, sc_ref, sh_ref, w_ref, v_ref):
    h = (x_ref[...] * (1.0 + sc_ref[0]) + sh_ref[0]).astype(BF16)
    for ct in range(D_MODEL // COL_T):
        c0 = ct * COL_T
        a = _bdot(h, w_ref[:, c0:c0 + COL_T])
        g = _bdot(h, w_ref[:, D_MODEL + c0:D_MODEL + c0 + COL_T])
        v_ref[:, c0:c0 + COL_T] = a * jax.nn.sigmoid(g)


def _pw1(x2, sc, sh, w_pw1):
    m = x2.shape[0]
    tpb = SEQ // PW_TM
    mod = pl.BlockSpec((1, 1, D_MODEL), lambda i: (i // tpb, 0, 0))
    return pl.pallas_call(
        _pw1_kernel,
        grid=(m // PW_TM,),
        in_specs=[
            pl.BlockSpec((PW_TM, D_MODEL), lambda i: (i, 0)),
            mod, mod,
            pl.BlockSpec((D_MODEL, 2 * D_MODEL), lambda i: (0, 0), pipeline_mode=pl.Buffered(1)),
        ],
        out_specs=pl.BlockSpec((PW_TM, D_MODEL), lambda i: (i, 0)),
        out_shape=jax.ShapeDtypeStruct((m, D_MODEL), F32),
        compiler_params=pltpu.CompilerParams(
            dimension_semantics=("parallel",), vmem_limit_bytes=VMEM_LIMIT),
        name="pw1",
    )(x2, sc, sh, w_pw1)


def _conv_kernel(v_ref, halo_ref, x_ref, gt_ref, wdw_ref, bdw_ref, gn_ref, bn_ref, w2_ref,
                 g_ref, b_ref, o_ref, vbuf, cbuf):
    tpb = SEQ // CV_TM
    first = (pl.program_id(0) % tpb) == 0
    vbuf[0:HALO, :] = jnp.where(first, 0.0, halo_ref[...])
    vbuf[HALO:, :] = v_ref[...]

    def col_chunk(c, carry):
        c0 = pl.multiple_of(c * LANES, LANES)
        acc = jnp.broadcast_to(bdw_ref[:, pl.ds(c0, LANES)], (CV_TM, LANES))
        for k in range(CONV_WIDTH):
            w = wdw_ref[k:k + 1, pl.ds(c0, LANES)]
            acc = acc + w * vbuf[k + 2:k + 2 + CV_TM, pl.ds(c0, LANES)]
        cbuf[:, pl.ds(c0, LANES)] = acc
        return carry

    lax.fori_loop(0, D_MODEL // LANES, col_chunk, 0)
    u = _layer_norm(cbuf[...], gn_ref[...], bn_ref[...])
    u = (u * jax.nn.sigmoid(u)).astype(BF16)
    y = _bdot(u, w2_ref[...])
    z = ALPHA * x_ref[...] + (1.0 + gt_ref[0]) * y
    o_ref[...] = _layer_norm(z, g_ref[...], b_ref[...])


def _conv(v2, x2, gt, w_dw, b_dw, g_norm, b_norm, w_pw2, g, b):
    m = x2.shape[0]
    tpb = SEQ // CV_TM
    hpt = CV_TM // HALO
    mod = pl.BlockSpec((1, 1, D_MODEL), lambda i: (i // tpb, 0, 0))
    vec = pl.BlockSpec((1, D_MODEL), lambda i: (0, 0))
    wdw = jnp.zeros((HALO, D_MODEL), F32).at[:CONV_WIDTH].set(w_dw)
    row = lambda a: a.reshape(1, D_MODEL)
    return pl.pallas_call(
        _conv_kernel,
        grid=(m // CV_TM,),
        in_specs=[
            pl.BlockSpec((CV_TM, D_MODEL), lambda i: (i, 0)),
            pl.BlockSpec((HALO, D_MODEL), lambda i: (jnp.maximum(i * hpt - 1, 0), 0)),
            pl.BlockSpec((CV_TM, D_MODEL), lambda i: (i, 0)),
            mod,
            pl.BlockSpec((HALO, D_MODEL), lambda i: (0, 0)),
            vec, vec, vec,
            pl.BlockSpec((D_MODEL, D_MODEL), lambda i: (0, 0), pipeline_mode=pl.Buffered(1)),
            vec, vec,
        ],
        out_specs=pl.BlockSpec((CV_TM, D_MODEL), lambda i: (i, 0)),
        out_shape=jax.ShapeDtypeStruct((m, D_MODEL), F32),
        scratch_shapes=[pltpu.VMEM((HALO + CV_TM, D_MODEL), F32), pltpu.VMEM((CV_TM, D_MODEL), F32)],
        compiler_params=pltpu.CompilerParams(
            dimension_semantics=("parallel",), vmem_limit_bytes=VMEM_LIMIT),
        name="conv",
    )(v2, v2, x2, gt, wdw, row(b_dw), row(g_norm), row(b_norm), w_pw2, row(g), row(b))


def kernel(x, c, ada_w, ada_b, ln_g, ln_b, s5_lam_re, s5_lam_im, s5_log_dt, s5_b_re, s5_b_im,
           s5_c_re, s5_c_im, s5_d, s5_w_glu, cv_w_pw1, cv_w_dw, cv_b_dw, cv_norm_g, cv_norm_b,
           cv_w_pw2, ffn_w_gu, ffn_w_down):
    mod = _ada(c, ada_w, ada_b)
    mods = [[mod[i, :, None, k * D_MODEL:(k + 1) * D_MODEL] for k in range(N_MOD)]
            for i in range(DEPTH)]

    sh_m, sc_m, gt_m, sh_f, sc_f, gt_f = mods[0]
    w_ui, w_out, coef = _s5_prep(s5_lam_re[0], s5_lam_im[0], s5_log_dt[0], s5_b_re[0], s5_b_im[0],
                                 s5_c_re[0], s5_c_im[0])
    xr = x.reshape(BATCH, ROWS, TAU * D_MODEL)
    ycat = _s5(xr, sc_m, sh_m, s5_d[0], w_ui, w_out, coef)
    x1 = _glu_ln(ycat, xr, gt_m, s5_w_glu[0].astype(BF16), ln_g[0, 0], ln_b[0, 0])
    x2 = x1.reshape(BATCH * SEQ, D_MODEL)
    x2 = _ffn(x2, sc_f, sh_f, gt_f, ffn_w_gu[0].astype(BF16), ffn_w_down[0].astype(BF16),
              ln_g[0, 1], ln_b[0, 1])

    sh_m, sc_m, gt_m, sh_f, sc_f, gt_f = mods[1]
    v2 = _pw1(x2, sc_m, sh_m, cv_w_pw1[0].astype(BF16))
    x2 = _conv(v2, x2, gt_m, cv_w_dw[0], cv_b_dw[0], cv_norm_g[0], cv_norm_b[0],
               cv_w_pw2[0].astype(BF16), ln_g[1, 0], ln_b[1, 0])
    x2 = _ffn(x2, sc_f, sh_f, gt_f, ffn_w_gu[1].astype(BF16), ffn_w_down[1].astype(BF16),
              ln_g[1, 1], ln_b[1, 1])
    return x2.reshape(BATCH, SEQ, D_MODEL)
```

```python
import functools
import math

import jax
import jax.numpy as jnp
from jax import lax
from jax.experimental import pallas as pl
from jax.experimental.pallas import tpu as pltpu

F32 = jnp.float32
BF16 = jnp.bfloat16

D_MODEL = 2048
BATCH = 4
SEQ = 8192
DEPTH = 2
S5_GROUP = 16
S5_GROUPS = D_MODEL // S5_GROUP
S5_STATE = 64
CONV_WIDTH = 31
D_FF = 5632
ALPHA = (2 * DEPTH) ** 0.25
LN_EPS = 1e-5
N_MOD = 6

LANES = 128
SUBLANES = 8
MXU_DIM = 256

TAU = 8
ROWS = SEQ // TAU
GPB = LANES // S5_GROUP
NBLK = D_MODEL // LANES
NPAIR = D_MODEL // MXU_DIM
KDIM = TAU * LANES
SDIM = GPB * S5_STATE * 2
S5_RT = 512
NSEG = SUBLANES
SEG = 66
S5_RP = NSEG * SEG
NSLAB = SDIM // LANES

VMEM_LIMIT = 56 * 1024 * 1024


def _layer_norm(z, g, b):
    mu = jnp.mean(z, axis=-1, keepdims=True)
    zc = z - mu
    var = jnp.mean(zc * zc, axis=-1, keepdims=True)
    return zc * lax.rsqrt(var + LN_EPS) * g + b


def _gelu_tanh(x):
    c = math.sqrt(2.0 / math.pi)
    return x * (0.5 * (1.0 + jnp.tanh(c * (x + 0.044715 * (x * x * x)))))


def _bdot(a, b):
    return jnp.dot(a, b, preferred_element_type=F32)


ADA_TN = 1024


def _ada_kernel(c_ref, w_ref, b_ref, o_ref):
    c = c_ref[...]
    ca = c * jax.nn.sigmoid(c)
    o_ref[0] = jnp.dot(ca, w_ref[0], preferred_element_type=F32,
                       precision=lax.Precision.HIGHEST) + b_ref[0]


def _ada(c, ada_w, ada_b):
    cp = jnp.zeros((SUBLANES, D_MODEL), F32).at[:BATCH].set(c)
    n = N_MOD * D_MODEL
    out = pl.pallas_call(
        _ada_kernel,
        grid=(DEPTH, n // ADA_TN),
        in_specs=[
            pl.BlockSpec((SUBLANES, D_MODEL), lambda i, j: (0, 0)),
            pl.BlockSpec((1, D_MODEL, ADA_TN), lambda i, j: (i, 0, j)),
            pl.BlockSpec((1, 1, ADA_TN), lambda i, j: (i, 0, j)),
        ],
        out_specs=pl.BlockSpec((1, SUBLANES, ADA_TN), lambda i, j: (i, 0, j)),
        out_shape=jax.ShapeDtypeStruct((DEPTH, SUBLANES, n), F32),
        compiler_params=pltpu.CompilerParams(
            dimension_semantics=("parallel", "parallel"), vmem_limit_bytes=VMEM_LIMIT),
        name="ada",
    )(cp, ada_w, ada_b.reshape(DEPTH, 1, n))
    return out[:, :BATCH]


def _cmul(a, b):
    return a[0] * b[0] - a[1] * b[1], a[0] * b[1] + a[1] * b[0]


def _s5_prep(lam_re, lam_im, log_dt, b_re, b_im, c_re, c_im):
    hp = lax.Precision.HIGHEST
    dt = jnp.exp(log_dt)[:, None]
    zr, zi = lam_re * dt, lam_im * dt
    er = jnp.exp(zr)
    a = (er * jnp.cos(zi), er * jnp.sin(zi))
    sh = jnp.sin(0.5 * zi)
    nr, ni = jnp.expm1(zr) * jnp.cos(zi) - 2.0 * sh * sh, a[1]
    den = lam_re * lam_re + lam_im * lam_im
    q = ((nr * lam_re + ni * lam_im) / den, (ni * lam_re - nr * lam_im) / den)
    bb = (q[0][..., None] * b_re - q[1][..., None] * b_im,
          q[0][..., None] * b_im + q[1][..., None] * b_re)
    pows = [(jnp.ones_like(zr), jnp.zeros_like(zr))]
    for _ in range(TAU):
        pows.append(_cmul(pows[-1], a))
    pr = jnp.stack([p[0] for p in pows])
    pi = jnp.stack([p[1] for p in pows])
    car = c_re[None] * pr[:, :, None, :] - c_im[None] * pi[:, :, None, :]
    cai = c_re[None] * pi[:, :, None, :] + c_im[None] * pr[:, :, None, :]
    kk = (jnp.einsum("kgop,gpi->kgoi", car[:TAU], bb[0], precision=hp)
          - jnp.einsum("kgop,gpi->kgoi", cai[:TAU], bb[1], precision=hp))
    g2 = (NBLK, GPB)

    def spread(compact, width):
        rep = jnp.tile(jnp.eye(width, dtype=BF16), (1, GPB))
        return jnp.einsum("...w,wy->...y", compact.astype(BF16), rep,
                          preferred_element_type=F32).astype(BF16)

    def group_mask(rows_per, cols_per):
        r = jnp.arange(GPB * rows_per)[:, None] // rows_per
        c = jnp.arange(GPB * cols_per)[None, :] // cols_per
        return (r == c).astype(BF16)

    kc = jnp.transpose(kk.reshape(TAU, *g2, S5_GROUP, S5_GROUP), (1, 2, 4, 0, 3))
    e = spread(kc.reshape(NBLK, LANES, TAU, S5_GROUP), S5_GROUP)
    e = e * group_mask(S5_GROUP, S5_GROUP)[:, None, :]
    w_toep = jnp.stack(
        [jnp.pad(e[:, :, :TAU - j, :].reshape(NBLK, LANES, (TAU - j) * LANES),
                 ((0, 0), (0, 0), (j * LANES, 0))) for j in range(TAU)], axis=1)
    w_toep = w_toep.reshape(NBLK, KDIM, KDIM)

    vr = pr[TAU - 1::-1][:, :, :, None] * bb[0][None] - pi[TAU - 1::-1][:, :, :, None] * bb[1][None]
    vi = pr[TAU - 1::-1][:, :, :, None] * bb[1][None] + pi[TAU - 1::-1][:, :, :, None] * bb[0][None]
    v = jnp.stack([vr, vi], axis=2)
    v = v.reshape(TAU, *g2, 2, S5_STATE, S5_GROUP)
    v = jnp.transpose(v, (1, 0, 2, 5, 3, 4))
    w_in = spread(v.reshape(NBLK, KDIM, 2, S5_STATE), S5_STATE)
    w_in = w_in * jnp.tile(group_mask(S5_GROUP, S5_STATE), (TAU, 1))[:, None, :]
    w_in = w_in.reshape(NBLK, KDIM, SDIM)

    co = jnp.stack([car[1:], -cai[1:]], axis=0)
    co = co.reshape(2, TAU, *g2, S5_GROUP, S5_STATE)
    co = jnp.transpose(co, (2, 0, 3, 5, 1, 4))
    w_out = spread(co.reshape(NBLK, SDIM, TAU, S5_GROUP), S5_GROUP)
    w_out = w_out * jnp.tile(group_mask(S5_STATE, S5_GROUP), (2, 1))[:, None, :]
    w_out = w_out.reshape(NBLK, SDIM, KDIM)

    a_t = pows[TAU]
    sq = [a_t]
    for _ in range(6):
        sq.append(_cmul(sq[-1], sq[-1]))
    assert SEG == 66
    a_seg = _cmul(sq[6], sq[1])
    coef = jnp.stack([a_t[0], a_t[1], a_seg[0], a_seg[1]], axis=0)
    coef = jnp.transpose(coef.reshape(4, NBLK, GPB * S5_STATE), (1, 0, 2))

    return w_toep, w_in, w_out, coef


def _s5_kernel(*refs):
    x_refs = refs[:TAU]
    (sc_ref, sh_ref, d_ref, wtoep_ref, win_ref, wout_ref, coef_ref, o_ref,
     x_scr, hp_scr, carry_scr, s_scr) = refs[TAU:]
    half = NSLAB // 2

    @pl.when(pl.program_id(2) == 0)
    def _():
        carry_scr[...] = jnp.zeros_like(carry_scr)

    for hf in range(2):
        ls = hf * LANES
        scale = 1.0 + sc_ref[0, :, ls:ls + LANES]
        shift = sh_ref[0, :, ls:ls + LANES]

        def h_of(j):
            return x_refs[j][0, :, ls:ls + LANES] * scale + shift

        u = jnp.concatenate([h_of(j).astype(BF16) for j in range(TAU)], axis=1)
        xin = _bdot(u, win_ref[hf])
        for l in range(NSLAB):
            x_scr[l, 0:S5_RT, :] = xin[:, l * LANES:(l + 1) * LANES]
            x_scr[l, S5_RT:S5_RP, :] = jnp.zeros((S5_RP - S5_RT, LANES), F32)

        def coef(row, l):
            return jnp.broadcast_to(coef_ref[hf, row:row + 1, l * LANES:(l + 1) * LANES],
                                    (NSEG, LANES))

        ar = [coef(0, l) for l in range(half)]
        ai = [coef(1, l) for l in range(half)]

        def step(k, hr, hi):
            nr, ni = [], []
            for l in range(half):
                xr = x_scr[l, pl.ds(k, NSEG, stride=SEG), :]
                xi = x_scr[l + half, pl.ds(k, NSEG, stride=SEG), :]
                nr.append(ar[l] * hr[l] - ai[l] * hi[l] + xr)
                ni.append(ar[l] * hi[l] + ai[l] * hr[l] + xi)
            return tuple(nr), tuple(ni)

        zero = tuple(jnp.zeros((NSEG, LANES), F32) for _ in range(half))
        er, ei = lax.fori_loop(0, SEG, lambda k, c: step(k, *c), (zero, zero))

        cr = [carry_scr[hf * NSLAB + l, 0:1, :] for l in range(half)]
        ci = [carry_scr[hf * NSLAB + half + l, 0:1, :] for l in range(half)]
        for s in range(NSEG):
            for l in range(half):
                s_scr[l, s:s + 1, :] = cr[l]
                s_scr[l + half, s:s + 1, :] = ci[l]
            if s + 1 < NSEG:
                for l in range(half):
                    gr = coef_ref[hf, 2:3, l * LANES:(l + 1) * LANES]
                    gi = coef_ref[hf, 3:4, l * LANES:(l + 1) * LANES]
                    nr = gr * cr[l] - gi * ci[l] + er[l][s:s + 1, :]
                    ni = gr * ci[l] + gi * cr[l] + ei[l][s:s + 1, :]
                    cr[l], ci[l] = nr, ni

        def step2(k, c):
            hr, hi = c
            for l in range(half):
                hp_scr[l, pl.ds(k, NSEG, stride=SEG), :] = hr[l]
                hp_scr[l + half, pl.ds(k, NSEG, stride=SEG), :] = hi[l]
            return step(k, hr, hi)

        start = (tuple(s_scr[l] for l in range(half)),
                 tuple(s_scr[l + half] for l in range(half)))
        lax.fori_loop(0, SEG, step2, start)
        for l in range(NSLAB):
            carry_scr[hf * NSLAB + l, 0:1, :] = hp_scr[l, S5_RT:S5_RT + 1, :]

        hp = jnp.concatenate([hp_scr[l, 0:S5_RT, :].astype(BF16) for l in range(NSLAB)], axis=1)
        dsk = d_ref[:, ls:ls + LANES]
        for q in range(KDIM // MXU_DIM):
            c0, kq = q * MXU_DIM, (q + 1) * MXU_DIM
            y = (_bdot(u[:, :kq], wtoep_ref[hf, :kq, c0:c0 + MXU_DIM])
                 + _bdot(hp, wout_ref[hf, :, c0:c0 + MXU_DIM]))
            for jj in range(MXU_DIM // LANES):
                j = q * (MXU_DIM // LANES) + jj
                yj = y[:, jj * LANES:(jj + 1) * LANES] + dsk * h_of(j)
                o_ref[0, 0, :, j * MXU_DIM + ls:j * MXU_DIM + ls + LANES] = (
                    _gelu_tanh(yj).astype(BF16))


def _s5(xr, sc, sh, d_skip, w_toep, w_in, w_out, coef):
    nper = D_MODEL // MXU_DIM
    x_specs = [
        pl.BlockSpec((1, S5_RT, MXU_DIM), functools.partial(
            lambda bp, b, rt, j: (b, rt, j * nper + bp), j=j))
        for j in range(TAU)
    ]
    mod_spec = pl.BlockSpec((1, 1, MXU_DIM), lambda bp, b, rt: (b, 0, bp))
    return pl.pallas_call(
        _s5_kernel,
        grid=(NPAIR, BATCH, ROWS // S5_RT),
        in_specs=x_specs + [
            mod_spec, mod_spec,
            pl.BlockSpec((1, MXU_DIM), lambda bp, b, rt: (0, bp)),
            pl.BlockSpec((2, KDIM, KDIM), lambda bp, b, rt: (bp, 0, 0),
                         pipeline_mode=pl.Buffered(1)),
            pl.BlockSpec((2, KDIM, SDIM), lambda bp, b, rt: (bp, 0, 0),
                         pipeline_mode=pl.Buffered(1)),
            pl.BlockSpec((2, SDIM, KDIM), lambda bp, b, rt: (bp, 0, 0),
                         pipeline_mode=pl.Buffered(1)),
            pl.BlockSpec((2, 4, SDIM // 2), lambda bp, b, rt: (bp, 0, 0)),
        ],
        out_specs=pl.BlockSpec((1, 1, S5_RT, TAU * MXU_DIM), lambda bp, b, rt: (b, bp, rt, 0)),
        out_shape=jax.ShapeDtypeStruct((BATCH, NPAIR, ROWS, TAU * MXU_DIM), BF16),
        scratch_shapes=[
            pltpu.VMEM((NSLAB, S5_RP, LANES), F32),
            pltpu.VMEM((NSLAB, S5_RP, LANES), F32),
            pltpu.VMEM((2 * NSLAB, SUBLANES, LANES), F32),
            pltpu.VMEM((NSLAB, NSEG, LANES), F32),
        ],
        compiler_params=pltpu.CompilerParams(
            dimension_semantics=("parallel", "parallel", "arbitrary"),
            vmem_limit_bytes=VMEM_LIMIT),
        name="s5",
    )(*([xr] * TAU), sc, sh, d_skip.reshape(1, D_MODEL), w_toep, w_in, w_out, coef)


GLU_RT = 512
COL_T = 512


def _glu_ln_kernel(y_ref, x_ref, gt_ref, w_ref, g_ref, b_ref, o_ref):
    y = jnp.concatenate([y_ref[0, bp] for bp in range(NPAIR)], axis=1)
    zs = []
    for ct in range(D_MODEL // COL_T):
        c0 = ct * COL_T
        v = _bdot(y, w_ref[:, c0:c0 + COL_T])
        g = _bdot(y, w_ref[:, D_MODEL + c0:D_MODEL + c0 + COL_T])
        gate = 1.0 + gt_ref[0, :, c0:c0 + COL_T]
        zs.append(ALPHA * x_ref[0, :, c0:c0 + COL_T] + gate * (v * jax.nn.sigmoid(g)))
    z = jnp.concatenate(zs, axis=1)
    o_ref[0] = _layer_norm(z, g_ref[...], b_ref[...])


def _glu_ln(ycat, xr, gt, w_glu, g, b):
    vec = pl.BlockSpec((1, D_MODEL), lambda bb, rt, j: (0, 0))
    return pl.pallas_call(
        _glu_ln_kernel,
        grid=(BATCH, ROWS // GLU_RT, TAU),
        in_specs=[
            pl.BlockSpec((1, NPAIR, GLU_RT, MXU_DIM), lambda bb, rt, j: (bb, 0, rt, j)),
            pl.BlockSpec((1, GLU_RT, D_MODEL), lambda bb, rt, j: (bb, rt, j)),
            pl.BlockSpec((1, 1, D_MODEL), lambda bb, rt, j: (bb, 0, 0)),
            pl.BlockSpec((D_MODEL, 2 * D_MODEL), lambda bb, rt, j: (0, 0),
                         pipeline_mode=pl.Buffered(1)),
            vec, vec,
        ],
        out_specs=pl.BlockSpec((1, GLU_RT, D_MODEL), lambda bb, rt, j: (bb, rt, j)),
        out_shape=jax.ShapeDtypeStruct((BATCH, ROWS, TAU * D_MODEL), F32),
        compiler_params=pltpu.CompilerParams(
            dimension_semantics=("parallel", "parallel", "parallel"),
            vmem_limit_bytes=VMEM_LIMIT),
        name="glu_ln",
    )(ycat, xr, gt, w_glu, g.reshape(1, D_MODEL), b.reshape(1, D_MODEL))


FFN_TM = 512
FFN_TF = 512


def _ffn_kernel(x_ref, sc_ref, sh_ref, gt_ref, wg_ref, wv_ref, wd_ref, g_ref, b_ref, o_ref,
                h_scr, acc_scr):
    f = pl.program_id(1)

    @pl.when(f == 0)
    def _():
        h_scr[...] = (x_ref[...] * (1.0 + sc_ref[0]) + sh_ref[0]).astype(BF16)
        acc_scr[...] = jnp.zeros_like(acc_scr)

    h = h_scr[...]
    g = _bdot(h, wg_ref[...])
    v = _bdot(h, wv_ref[...])
    a = (g * jax.nn.sigmoid(g) * v).astype(BF16)
    acc_scr[...] += _bdot(a, wd_ref[...])

    @pl.when(f == pl.num_programs(1) - 1)
    def _():
        z = ALPHA * x_ref[...] + (1.0 + gt_ref[0]) * acc_scr[...]
        o_ref[...] = _layer_norm(z, g_ref[...], b_ref[...])


def _ffn(x2, sc, sh, gt, w_gu, w_down, g, b):
    m = x2.shape[0]
    tpb = SEQ // FFN_TM
    nf = D_FF // FFN_TF
    mod = pl.BlockSpec((1, 1, D_MODEL), lambda i, f: (i // tpb, 0, 0))
    vec = pl.BlockSpec((1, D_MODEL), lambda i, f: (0, 0))
    return pl.pallas_call(
        _ffn_kernel,
        grid=(m // FFN_TM, nf),
        in_specs=[
            pl.BlockSpec((FFN_TM, D_MODEL), lambda i, f: (i, 0)),
            mod, mod, mod,
            pl.BlockSpec((D_MODEL, FFN_TF), lambda i, f: (0, f)),
            pl.BlockSpec((D_MODEL, FFN_TF), lambda i, f: (0, nf + f)),
            pl.BlockSpec((FFN_TF, D_MODEL), lambda i, f: (f, 0)),
            vec, vec,
        ],
        out_specs=pl.BlockSpec((FFN_TM, D_MODEL), lambda i, f: (i, 0)),
        out_shape=jax.ShapeDtypeStruct((m, D_MODEL), F32),
        scratch_shapes=[pltpu.VMEM((FFN_TM, D_MODEL), BF16), pltpu.VMEM((FFN_TM, D_MODEL), F32)],
        compiler_params=pltpu.CompilerParams(
            dimension_semantics=("parallel", "arbitrary"), vmem_limit_bytes=VMEM_LIMIT),
        name="ffn",
    )(x2, sc, sh, gt, w_gu, w_gu, w_down, g.reshape(1, D_MODEL), b.reshape(1, D_MODEL))


PW_TM = 512
CV_TM = 256
CV_STRIDE = 4
CV_BAND = SUBLANES * CV_STRIDE
HALO = 32


def _pw1_kernel(x_ref, sc_ref, sh_ref, w_ref, v_ref):
    h = (x_ref[...] * (1.0 + sc_ref[0]) + sh_ref[0]).astype(BF16)
    for ct in range(D_MODEL // COL_T):
        c0 = ct * COL_T
        a = _bdot(h, w_ref[:, c0:c0 + COL_T])
        g = _bdot(h, w_ref[:, D_MODEL + c0:D_MODEL + c0 + COL_T])
        v_ref[:, c0:c0 + COL_T] = a * jax.nn.sigmoid(g)


def _pw1(x2, sc, sh, w_pw1):
    m = x2.shape[0]
    tpb = SEQ // PW_TM
    mod = pl.BlockSpec((1, 1, D_MODEL), lambda i: (i // tpb, 0, 0))
    return pl.pallas_call(
        _pw1_kernel,
        grid=(m // PW_TM,),
        in_specs=[
            pl.BlockSpec((PW_TM, D_MODEL), lambda i: (i, 0)),
            mod, mod,
            pl.BlockSpec((D_MODEL, 2 * D_MODEL), lambda i: (0, 0), pipeline_mode=pl.Buffered(1)),
        ],
        out_specs=pl.BlockSpec((PW_TM, D_MODEL), lambda i: (i, 0)),
        out_shape=jax.ShapeDtypeStruct((m, D_MODEL), F32),
        compiler_params=pltpu.CompilerParams(
            dimension_semantics=("parallel",), vmem_limit_bytes=VMEM_LIMIT),
        name="pw1",
    )(x2, sc, sh, w_pw1)


def _conv_kernel(v_ref, halo_ref, x_ref, gt_ref, wdw_ref, bdw_ref, gn_ref, bn_ref, w2_ref,
                 g_ref, b_ref, o_ref, vbuf, cbuf):
    tpb = SEQ // CV_TM
    first = (pl.program_id(0) % tpb) == 0
    for l in range(NBLK):
        vbuf[l, 0:HALO, :] = jnp.where(first, 0.0, halo_ref[:, l * LANES:(l + 1) * LANES])
        vbuf[l, HALO:, :] = v_ref[:, l * LANES:(l + 1) * LANES]

    def slab(l, carry):
        w = [jnp.broadcast_to(wdw_ref[l, k:k + 1, :], (SUBLANES, LANES)) for k in range(CONV_WIDTH)]
        bias = jnp.broadcast_to(bdw_ref[l], (SUBLANES, LANES))
        for band in range(CV_TM // CV_BAND):
            base = band * CV_BAND
            taps = {}

            def tap(r):
                if r not in taps:
                    taps[r] = vbuf[l, pl.ds(base + r, SUBLANES, stride=CV_STRIDE), :]
                return taps[r]

            for t0 in range(CV_STRIDE):
                acc = bias
                for k in range(CONV_WIDTH):
                    acc = acc + w[k] * tap(t0 + k + 2)
                cbuf[l, pl.ds(base + t0, SUBLANES, stride=CV_STRIDE), :] = acc
        return carry

    lax.fori_loop(0, NBLK, slab, 0)
    conv = jnp.concatenate([cbuf[l] for l in range(NBLK)], axis=1)
    u = _layer_norm(conv, gn_ref[...], bn_ref[...])
    u = (u * jax.nn.sigmoid(u)).astype(BF16)
    y = _bdot(u, w2_ref[...])
    z = ALPHA * x_ref[...] + (1.0 + gt_ref[0]) * y
    o_ref[...] = _layer_norm(z, g_ref[...], b_ref[...])


def _conv(v2, x2, gt, w_dw, b_dw, g_norm, b_norm, w_pw2, g, b):
    m = x2.shape[0]
    tpb = SEQ // CV_TM
    hpt = CV_TM // HALO
    mod = pl.BlockSpec((1, 1, D_MODEL), lambda i: (i // tpb, 0, 0))
    vec = pl.BlockSpec((1, D_MODEL), lambda i: (0, 0))
    wdw = jnp.zeros((HALO, D_MODEL), F32).at[:CONV_WIDTH].set(w_dw)
    wdw = jnp.transpose(wdw.reshape(HALO, NBLK, LANES), (1, 0, 2))
    bdw = b_dw.reshape(NBLK, 1, LANES)
    row = lambda a: a.reshape(1, D_MODEL)
    return pl.pallas_call(
        _conv_kernel,
        grid=(m // CV_TM,),
        in_specs=[
            pl.BlockSpec((CV_TM, D_MODEL), lambda i: (i, 0)),
            pl.BlockSpec((HALO, D_MODEL), lambda i: (jnp.maximum(i * hpt - 1, 0), 0)),
            pl.BlockSpec((CV_TM, D_MODEL), lambda i: (i, 0)),
            mod,
            pl.BlockSpec((NBLK, HALO, LANES), lambda i: (0, 0, 0)),
            pl.BlockSpec((NBLK, 1, LANES), lambda i: (0, 0, 0)),
            vec, vec,
            pl.BlockSpec((D_MODEL, D_MODEL), lambda i: (0, 0), pipeline_mode=pl.Buffered(1)),
            vec, vec,
        ],
        out_specs=pl.BlockSpec((CV_TM, D_MODEL), lambda i: (i, 0)),
        out_shape=jax.ShapeDtypeStruct((m, D_MODEL), F32),
        scratch_shapes=[pltpu.VMEM((NBLK, HALO + CV_TM, LANES), F32),
                        pltpu.VMEM((NBLK, CV_TM, LANES), F32)],
        compiler_params=pltpu.CompilerParams(
            dimension_semantics=("parallel",), vmem_limit_bytes=VMEM_LIMIT),
        name="conv",
    )(v2, v2, x2, gt, wdw, bdw, row(g_norm), row(b_norm), w_pw2, row(g), row(b))


def kernel(x, c, ada_w, ada_b, ln_g, ln_b, s5_lam_re, s5_lam_im, s5_log_dt, s5_b_re, s5_b_im,
           s5_c_re, s5_c_im, s5_d, s5_w_glu, cv_w_pw1, cv_w_dw, cv_b_dw, cv_norm_g, cv_norm_b,
           cv_w_pw2, ffn_w_gu, ffn_w_down):
    mod = _ada(c, ada_w, ada_b)
    mods = [[mod[i, :, None, k * D_MODEL:(k + 1) * D_MODEL] for k in range(N_MOD)]
            for i in range(DEPTH)]

    sh_m, sc_m, gt_m, sh_f, sc_f, gt_f = mods[0]
    s5_ops = _s5_prep(s5_lam_re[0], s5_lam_im[0], s5_log_dt[0], s5_b_re[0], s5_b_im[0],
                                 s5_c_re[0], s5_c_im[0])
    xr = x.reshape(BATCH, ROWS, TAU * D_MODEL)
    ycat = _s5(xr, sc_m, sh_m, s5_d[0], *s5_ops)
    x1 = _glu_ln(ycat, xr, gt_m, s5_w_glu[0].astype(BF16), ln_g[0, 0], ln_b[0, 0])
    x2 = x1.reshape(BATCH * SEQ, D_MODEL)
    x2 = _ffn(x2, sc_f, sh_f, gt_f, ffn_w_gu[0].astype(BF16), ffn_w_down[0].astype(BF16),
              ln_g[0, 1], ln_b[0, 1])

    sh_m, sc_m, gt_m, sh_f, sc_f, gt_f = mods[1]
    v2 = _pw1(x2, sc_m, sh_m, cv_w_pw1[0].astype(BF16))
    x2 = _conv(v2, x2, gt_m, cv_w_dw[0], cv_b_dw[0], cv_norm_g[0], cv_norm_b[0],
               cv_w_pw2[0].astype(BF16), ln_g[1, 0], ln_b[1, 0])
    x2 = _ffn(x2, sc_f, sh_f, gt_f, ffn_w_gu[1].astype(BF16), ffn_w_down[1].astype(BF16),
              ln_g[1, 1], ln_b[1, 1])
    return x2.reshape(BATCH, SEQ, D_MODEL)
```

```python
import functools
import math

import jax
import jax.numpy as jnp
from jax import lax
from jax.experimental import pallas as pl
from jax.experimental.pallas import tpu as pltpu

F32 = jnp.float32
BF16 = jnp.bfloat16

D_MODEL = 2048
BATCH = 4
SEQ = 8192
DEPTH = 2
S5_GROUP = 16
S5_GROUPS = D_MODEL // S5_GROUP
S5_STATE = 64
CONV_WIDTH = 31
D_FF = 5632
ALPHA = (2 * DEPTH) ** 0.25
LN_EPS = 1e-5
N_MOD = 6

LANES = 128
SUBLANES = 8
MXU_DIM = 256

TAU = 8
ROWS = SEQ // TAU
GPB = LANES // S5_GROUP
NBLK = D_MODEL // LANES
NPAIR = D_MODEL // MXU_DIM
KDIM = TAU * LANES
SDIM = GPB * S5_STATE * 2
S5_RT = 512
NSEG = SUBLANES
SEG = 66
S5_RP = NSEG * SEG
NSLAB = SDIM // LANES

VMEM_LIMIT = 56 * 1024 * 1024


def _layer_norm(z, g, b):
    mu = jnp.mean(z, axis=-1, keepdims=True)
    zc = z - mu
    var = jnp.mean(zc * zc, axis=-1, keepdims=True)
    return zc * lax.rsqrt(var + LN_EPS) * g + b


def _gelu_tanh(x):
    c = math.sqrt(2.0 / math.pi)
    return x * (0.5 * (1.0 + jnp.tanh(c * (x + 0.044715 * (x * x * x)))))


def _bdot(a, b):
    return jnp.dot(a, b, preferred_element_type=F32)


ADA_TN = 1024


def _ada_kernel(c_ref, w_ref, b_ref, o_ref):
    c = c_ref[...]
    ca = c * jax.nn.sigmoid(c)
    o_ref[0] = jnp.dot(ca, w_ref[0], preferred_element_type=F32,
                       precision=lax.Precision.HIGHEST) + b_ref[0]


def _ada(c, ada_w, ada_b):
    cp = jnp.zeros((SUBLANES, D_MODEL), F32).at[:BATCH].set(c)
    n = N_MOD * D_MODEL
    out = pl.pallas_call(
        _ada_kernel,
        grid=(DEPTH, n // ADA_TN),
        in_specs=[
            pl.BlockSpec((SUBLANES, D_MODEL), lambda i, j: (0, 0)),
            pl.BlockSpec((1, D_MODEL, ADA_TN), lambda i, j: (i, 0, j)),
            pl.BlockSpec((1, 1, ADA_TN), lambda i, j: (i, 0, j)),
        ],
        out_specs=pl.BlockSpec((1, SUBLANES, ADA_TN), lambda i, j: (i, 0, j)),
        out_shape=jax.ShapeDtypeStruct((DEPTH, SUBLANES, n), F32),
        compiler_params=pltpu.CompilerParams(
            dimension_semantics=("parallel", "parallel"), vmem_limit_bytes=VMEM_LIMIT),
        name="ada",
    )(cp, ada_w, ada_b.reshape(DEPTH, 1, n))
    return out[:, :BATCH]


def _cmul(a, b):
    return a[0] * b[0] - a[1] * b[1], a[0] * b[1] + a[1] * b[0]


def _s5_prep(lam_re, lam_im, log_dt, b_re, b_im, c_re, c_im):
    hp = lax.Precision.HIGHEST
    dt = jnp.exp(log_dt)[:, None]
    zr, zi = lam_re * dt, lam_im * dt
    er = jnp.exp(zr)
    a = (er * jnp.cos(zi), er * jnp.sin(zi))
    sh = jnp.sin(0.5 * zi)
    nr, ni = jnp.expm1(zr) * jnp.cos(zi) - 2.0 * sh * sh, a[1]
    den = lam_re * lam_re + lam_im * lam_im
    q = ((nr * lam_re + ni * lam_im) / den, (ni * lam_re - nr * lam_im) / den)
    bb = (q[0][..., None] * b_re - q[1][..., None] * b_im,
          q[0][..., None] * b_im + q[1][..., None] * b_re)
    pows = [(jnp.ones_like(zr), jnp.zeros_like(zr))]
    for _ in range(TAU):
        pows.append(_cmul(pows[-1], a))
    pr = jnp.stack([p[0] for p in pows])
    pi = jnp.stack([p[1] for p in pows])
    car = c_re[None] * pr[:, :, None, :] - c_im[None] * pi[:, :, None, :]
    cai = c_re[None] * pi[:, :, None, :] + c_im[None] * pr[:, :, None, :]
    kk = (jnp.einsum("kgop,gpi->kgoi", car[:TAU], bb[0], precision=hp)
          - jnp.einsum("kgop,gpi->kgoi", cai[:TAU], bb[1], precision=hp))
    g2 = (NBLK, GPB)

    def spread(compact, width):
        rep = jnp.tile(jnp.eye(width, dtype=BF16), (1, GPB))
        return jnp.einsum("...w,wy->...y", compact.astype(BF16), rep,
                          preferred_element_type=F32).astype(BF16)

    def group_mask(rows_per, cols_per):
        r = jnp.arange(GPB * rows_per)[:, None] // rows_per
        c = jnp.arange(GPB * cols_per)[None, :] // cols_per
        return (r == c).astype(BF16)

    kc = jnp.transpose(kk.reshape(TAU, *g2, S5_GROUP, S5_GROUP), (0, 1, 2, 4, 3))
    e = spread(kc.reshape(TAU, NBLK, LANES, S5_GROUP), S5_GROUP) * group_mask(S5_GROUP, S5_GROUP)
    zero = jnp.zeros((NBLK, LANES, LANES), BF16)
    w_toep = jnp.concatenate(
        [jnp.concatenate([e[jo - j] if jo >= j else zero for jo in range(TAU)], axis=2)
         for j in range(TAU)], axis=1)

    vr = pr[TAU - 1::-1][:, :, :, None] * bb[0][None] - pi[TAU - 1::-1][:, :, :, None] * bb[1][None]
    vi = pr[TAU - 1::-1][:, :, :, None] * bb[1][None] + pi[TAU - 1::-1][:, :, :, None] * bb[0][None]
    in_mask = jnp.tile(group_mask(S5_GROUP, S5_STATE), (TAU, 1))

    def in_part(vp):
        vp = jnp.transpose(vp.reshape(TAU, *g2, S5_STATE, S5_GROUP), (1, 0, 2, 4, 3))
        return spread(vp.reshape(NBLK, KDIM, S5_STATE), S5_STATE) * in_mask

    w_in = jnp.concatenate([in_part(vr), in_part(vi)], axis=2)

    co = jnp.stack([car[1:], -cai[1:]], axis=0)
    co = co.reshape(2, TAU, *g2, S5_GROUP, S5_STATE)
    co = jnp.transpose(co, (1, 2, 0, 3, 5, 4))
    out_mask = jnp.tile(group_mask(S5_STATE, S5_GROUP), (2, 1))
    wo = spread(co.reshape(TAU, NBLK, SDIM, S5_GROUP), S5_GROUP) * out_mask
    w_out = jnp.concatenate([wo[jo] for jo in range(TAU)], axis=2)

    a_t = pows[TAU]
    sq = [a_t]
    for _ in range(6):
        sq.append(_cmul(sq[-1], sq[-1]))
    assert SEG == 66
    a_seg = _cmul(sq[6], sq[1])
    coef = jnp.stack([a_t[0], a_t[1], a_seg[0], a_seg[1]], axis=0)
    coef = jnp.transpose(coef.reshape(4, NBLK, GPB * S5_STATE), (1, 0, 2))

    return w_toep, w_in, w_out, coef


def _s5_kernel(*refs):
    x_refs = refs[:TAU]
    (sc_ref, sh_ref, d_ref, wtoep_ref, win_ref, wout_ref, coef_ref, o_ref,
     x_all, hp_all, carry_scr, s_all) = refs[TAU:]
    half = NSLAB // 2

    @pl.when(pl.program_id(2) == 0)
    def _():
        carry_scr[...] = jnp.zeros_like(carry_scr)

    for hf in range(2):
        x_scr, hp_scr, s_scr = x_all.at[hf], hp_all.at[hf], s_all.at[hf]
        ls = hf * LANES
        scale = 1.0 + sc_ref[0, :, ls:ls + LANES]
        shift = sh_ref[0, :, ls:ls + LANES]

        def h_of(j):
            return x_refs[j][0, :, ls:ls + LANES] * scale + shift

        u = jnp.concatenate([h_of(j).astype(BF16) for j in range(TAU)], axis=1)
        xin = _bdot(u, win_ref[hf])
        for l in range(NSLAB):
            x_scr[l, 0:S5_RT, :] = xin[:, l * LANES:(l + 1) * LANES]
            x_scr[l, S5_RT:S5_RP, :] = jnp.zeros((S5_RP - S5_RT, LANES), F32)

        def coef(row, l):
            return jnp.broadcast_to(coef_ref[hf, row:row + 1, l * LANES:(l + 1) * LANES],
                                    (NSEG, LANES))

        ar = [coef(0, l) for l in range(half)]
        ai = [coef(1, l) for l in range(half)]

        def step(k, hr, hi):
            nr, ni = [], []
            for l in range(half):
                xr = x_scr[l, pl.ds(k, NSEG, stride=SEG), :]
                xi = x_scr[l + half, pl.ds(k, NSEG, stride=SEG), :]
                nr.append(ar[l] * hr[l] - ai[l] * hi[l] + xr)
                ni.append(ar[l] * hi[l] + ai[l] * hr[l] + xi)
            return tuple(nr), tuple(ni)

        zero = tuple(jnp.zeros((NSEG, LANES), F32) for _ in range(half))
        er, ei = zero, zero
        for k in range(SEG):
            er, ei = step(k, er, ei)

        cr = [carry_scr[hf * NSLAB + l, 0:1, :] for l in range(half)]
        ci = [carry_scr[hf * NSLAB + half + l, 0:1, :] for l in range(half)]
        for s in range(NSEG):
            for l in range(half):
                s_scr[l, s:s + 1, :] = cr[l]
                s_scr[l + half, s:s + 1, :] = ci[l]
            if s + 1 < NSEG:
                for l in range(half):
                    gr = coef_ref[hf, 2:3, l * LANES:(l + 1) * LANES]
                    gi = coef_ref[hf, 3:4, l * LANES:(l + 1) * LANES]
                    nr = gr * cr[l] - gi * ci[l] + er[l][s:s + 1, :]
                    ni = gr * ci[l] + gi * cr[l] + ei[l][s:s + 1, :]
                    cr[l], ci[l] = nr, ni

        hr = tuple(s_scr[l] for l in range(half))
        hi = tuple(s_scr[l + half] for l in range(half))
        for k in range(SEG):
            for l in range(half):
                hp_scr[l, pl.ds(k, NSEG, stride=SEG), :] = hr[l]
                hp_scr[l + half, pl.ds(k, NSEG, stride=SEG), :] = hi[l]
            hr, hi = step(k, hr, hi)
        for l in range(NSLAB):
            carry_scr[hf * NSLAB + l, 0:1, :] = hp_scr[l, S5_RT:S5_RT + 1, :]

        hp = jnp.concatenate([hp_scr[l, 0:S5_RT, :].astype(BF16) for l in range(NSLAB)], axis=1)
        dsk = d_ref[:, ls:ls + LANES]
        for q in range(KDIM // MXU_DIM):
            c0, kq = q * MXU_DIM, (q + 1) * MXU_DIM
            y = (_bdot(u[:, :kq], wtoep_ref[hf, :kq, c0:c0 + MXU_DIM])
                 + _bdot(hp, wout_ref[hf, :, c0:c0 + MXU_DIM]))
            for jj in range(MXU_DIM // LANES):
                j = q * (MXU_DIM // LANES) + jj
                yj = y[:, jj * LANES:(jj + 1) * LANES] + dsk * h_of(j)
                o_ref[0, 0, :, j * MXU_DIM + ls:j * MXU_DIM + ls + LANES] = (
                    _gelu_tanh(yj).astype(BF16))


def _s5(xr, sc, sh, d_skip, w_toep, w_in, w_out, coef):
    nper = D_MODEL // MXU_DIM
    x_specs = [
        pl.BlockSpec((1, S5_RT, MXU_DIM), functools.partial(
            lambda bp, b, rt, j: (b, rt, j * nper + bp), j=j))
        for j in range(TAU)
    ]
    mod_spec = pl.BlockSpec((1, 1, MXU_DIM), lambda bp, b, rt: (b, 0, bp))
    return pl.pallas_call(
        _s5_kernel,
        grid=(NPAIR, BATCH, ROWS // S5_RT),
        in_specs=x_specs + [
            mod_spec, mod_spec,
            pl.BlockSpec((1, MXU_DIM), lambda bp, b, rt: (0, bp)),
            pl.BlockSpec((2, KDIM, KDIM), lambda bp, b, rt: (bp, 0, 0),
                         pipeline_mode=pl.Buffered(1)),
            pl.BlockSpec((2, KDIM, SDIM), lambda bp, b, rt: (bp, 0, 0),
                         pipeline_mode=pl.Buffered(1)),
            pl.BlockSpec((2, SDIM, KDIM), lambda bp, b, rt: (bp, 0, 0),
                         pipeline_mode=pl.Buffered(1)),
            pl.BlockSpec((2, 4, SDIM // 2), lambda bp, b, rt: (bp, 0, 0)),
        ],
        out_specs=pl.BlockSpec((1, 1, S5_RT, TAU * MXU_DIM), lambda bp, b, rt: (b, bp, rt, 0)),
        out_shape=jax.ShapeDtypeStruct((BATCH, NPAIR, ROWS, TAU * MXU_DIM), BF16),
        scratch_shapes=[
            pltpu.VMEM((2, NSLAB, S5_RP, LANES), F32),
            pltpu.VMEM((2, NSLAB, S5_RP, LANES), F32),
            pltpu.VMEM((2 * NSLAB, SUBLANES, LANES), F32),
            pltpu.VMEM((2, NSLAB, NSEG, LANES), F32),
        ],
        compiler_params=pltpu.CompilerParams(
            dimension_semantics=("parallel", "parallel", "arbitrary"),
            vmem_limit_bytes=VMEM_LIMIT),
        name="s5",
    )(*([xr] * TAU), sc, sh, d_skip.reshape(1, D_MODEL), w_toep, w_in, w_out, coef)


GLU_RT = 512
COL_T = 512


def _glu_ln_kernel(y_ref, x_ref, gt_ref, w_ref, g_ref, b_ref, o_ref):
    y = jnp.concatenate([y_ref[0, bp] for bp in range(NPAIR)], axis=1)
    zs = []
    for ct in range(D_MODEL // COL_T):
        c0 = ct * COL_T
        v = _bdot(y, w_ref[:, c0:c0 + COL_T])
        g = _bdot(y, w_ref[:, D_MODEL + c0:D_MODEL + c0 + COL_T])
        gate = 1.0 + gt_ref[0, :, c0:c0 + COL_T]
        zs.append(ALPHA * x_ref[0, :, c0:c0 + COL_T] + gate * (v * jax.nn.sigmoid(g)))
    z = jnp.concatenate(zs, axis=1)
    o_ref[0] = _layer_norm(z, g_ref[...], b_ref[...])


def _glu_ln(ycat, xr, gt, w_glu, g, b):
    vec = pl.BlockSpec((1, D_MODEL), lambda bb, rt, j: (0, 0))
    return pl.pallas_call(
        _glu_ln_kernel,
        grid=(BATCH, ROWS // GLU_RT, TAU),
        in_specs=[
            pl.BlockSpec((1, NPAIR, GLU_RT, MXU_DIM), lambda bb, rt, j: (bb, 0, rt, j)),
            pl.BlockSpec((1, GLU_RT, D_MODEL), lambda bb, rt, j: (bb, rt, j)),
            pl.BlockSpec((1, 1, D_MODEL), lambda bb, rt, j: (bb, 0, 0)),
            pl.BlockSpec((D_MODEL, 2 * D_MODEL), lambda bb, rt, j: (0, 0),
                         pipeline_mode=pl.Buffered(1)),
            vec, vec,
        ],
        out_specs=pl.BlockSpec((1, GLU_RT, D_MODEL), lambda bb, rt, j: (bb, rt, j)),
        out_shape=jax.ShapeDtypeStruct((BATCH, ROWS, TAU * D_MODEL), F32),
        compiler_params=pltpu.CompilerParams(
            dimension_semantics=("parallel", "parallel", "parallel"),
            vmem_limit_bytes=VMEM_LIMIT),
        name="glu_ln",
    )(ycat, xr, gt, w_glu, g.reshape(1, D_MODEL), b.reshape(1, D_MODEL))


FFN_TM = 512
FFN_TF = 512


def _ffn_kernel(x_ref, sc_ref, sh_ref, gt_ref, wg_ref, wv_ref, wd_ref, g_ref, b_ref, o_ref,
                h_scr, acc_scr):
    f = pl.program_id(1)

    @pl.when(f == 0)
    def _():
        h_scr[...] = (x_ref[...] * (1.0 + sc_ref[0]) + sh_ref[0]).astype(BF16)
        acc_scr[...] = jnp.zeros_like(acc_scr)

    h = h_scr[...]
    g = _bdot(h, wg_ref[...])
    v = _bdot(h, wv_ref[...])
    a = (g * jax.nn.sigmoid(g) * v).astype(BF16)
    acc_scr[...] += _bdot(a, wd_ref[...])

    @pl.when(f == pl.num_programs(1) - 1)
    def _():
        z = ALPHA * x_ref[...] + (1.0 + gt_ref[0]) * acc_scr[...]
        o_ref[...] = _layer_norm(z, g_ref[...], b_ref[...])


def _ffn(x2, sc, sh, gt, w_gu, w_down, g, b):
    m = x2.shape[0]
    tpb = SEQ // FFN_TM
    nf = D_FF // FFN_TF
    mod = pl.BlockSpec((1, 1, D_MODEL), lambda i, f: (i // tpb, 0, 0))
    vec = pl.BlockSpec((1, D_MODEL), lambda i, f: (0, 0))
    return pl.pallas_call(
        _ffn_kernel,
        grid=(m // FFN_TM, nf),
        in_specs=[
            pl.BlockSpec((FFN_TM, D_MODEL), lambda i, f: (i, 0)),
            mod, mod, mod,
            pl.BlockSpec((D_MODEL, FFN_TF), lambda i, f: (0, f)),
            pl.BlockSpec((D_MODEL, FFN_TF), lambda i, f: (0, nf + f)),
            pl.BlockSpec((FFN_TF, D_MODEL), lambda i, f: (f, 0)),
            vec, vec,
        ],
        out_specs=pl.BlockSpec((FFN_TM, D_MODEL), lambda i, f: (i, 0)),
        out_shape=jax.ShapeDtypeStruct((m, D_MODEL), F32),
        scratch_shapes=[pltpu.VMEM((FFN_TM, D_MODEL), BF16), pltpu.VMEM((FFN_TM, D_MODEL), F32)],
        compiler_params=pltpu.CompilerParams(
            dimension_semantics=("parallel", "arbitrary"), vmem_limit_bytes=VMEM_LIMIT),
        name="ffn",
    )(x2, sc, sh, gt, w_gu, w_gu, w_down, g.reshape(1, D_MODEL), b.reshape(1, D_MODEL))


CV_TM = 256
CV_STRIDE = 4
CV_BAND = SUBLANES * CV_STRIDE
HALO = 32


def _conv_kernel(x_ref, sc_ref, sh_ref, gt_ref, w1_ref, wdw_ref, bdw_ref, gn_ref, bn_ref, w2_ref,
                 g_ref, b_ref, o_ref, vbuf, cbuf):
    tpb = SEQ // CV_TM

    @pl.when(pl.program_id(0) % tpb == 0)
    def _():
        vbuf[:, 0:HALO, :] = jnp.zeros((NBLK, HALO, LANES), F32)

    def conv_slab(l):
        w = [jnp.broadcast_to(wdw_ref[l, k:k + 1, :], (SUBLANES, LANES)) for k in range(CONV_WIDTH)]
        bias = jnp.broadcast_to(bdw_ref[l], (SUBLANES, LANES))
        for band in range(CV_TM // CV_BAND):
            base = band * CV_BAND
            acc = [bias] * CV_STRIDE
            for r in range(2, CONV_WIDTH + 1 + CV_STRIDE):
                tap = vbuf[l, pl.ds(base + r, SUBLANES, stride=CV_STRIDE), :]
                for t0 in range(CV_STRIDE):
                    k = r - t0 - 2
                    if 0 <= k < CONV_WIDTH:
                        acc[t0] = acc[t0] + w[k] * tap
            for t0 in range(CV_STRIDE):
                cbuf[l, pl.ds(base + t0, SUBLANES, stride=CV_STRIDE), :] = acc[t0]
        vbuf[l, 0:HALO, :] = vbuf[l, CV_TM:CV_TM + HALO, :]

    h = (x_ref[...] * (1.0 + sc_ref[0]) + sh_ref[0]).astype(BF16)
    for ct in range(D_MODEL // COL_T):
        c0 = ct * COL_T
        a = _bdot(h, w1_ref[:, c0:c0 + COL_T])
        gl = _bdot(h, w1_ref[:, D_MODEL + c0:D_MODEL + c0 + COL_T])
        v = a * jax.nn.sigmoid(gl)
        for s in range(COL_T // LANES):
            l = ct * (COL_T // LANES) + s
            vbuf[l, HALO:, :] = v[:, s * LANES:(s + 1) * LANES]
            conv_slab(l)

    conv = jnp.concatenate([cbuf[l] for l in range(NBLK)], axis=1)
    u = _layer_norm(conv, gn_ref[...], bn_ref[...])
    u = (u * jax.nn.sigmoid(u)).astype(BF16)
    y = _bdot(u, w2_ref[...])
    z = ALPHA * x_ref[...] + (1.0 + gt_ref[0]) * y
    o_ref[...] = _layer_norm(z, g_ref[...], b_ref[...])


def _conv(x2, sc, sh, gt, w_pw1, w_dw, b_dw, g_norm, b_norm, w_pw2, g, b):
    m = x2.shape[0]
    tpb = SEQ // CV_TM
    mod = pl.BlockSpec((1, 1, D_MODEL), lambda i: (i // tpb, 0, 0))
    vec = pl.BlockSpec((1, D_MODEL), lambda i: (0, 0))
    wdw = jnp.zeros((HALO, D_MODEL), F32).at[:CONV_WIDTH].set(w_dw)
    wdw = jnp.transpose(wdw.reshape(HALO, NBLK, LANES), (1, 0, 2))
    bdw = b_dw.reshape(NBLK, 1, LANES)
    row = lambda a: a.reshape(1, D_MODEL)
    return pl.pallas_call(
        _conv_kernel,
        grid=(m // CV_TM,),
        in_specs=[
            pl.BlockSpec((CV_TM, D_MODEL), lambda i: (i, 0)),
            mod, mod, mod,
            pl.BlockSpec((D_MODEL, 2 * D_MODEL), lambda i: (0, 0), pipeline_mode=pl.Buffered(1)),
            pl.BlockSpec((NBLK, HALO, LANES), lambda i: (0, 0, 0)),
            pl.BlockSpec((NBLK, 1, LANES), lambda i: (0, 0, 0)),
            vec, vec,
            pl.BlockSpec((D_MODEL, D_MODEL), lambda i: (0, 0), pipeline_mode=pl.Buffered(1)),
            vec, vec,
        ],
        out_specs=pl.BlockSpec((CV_TM, D_MODEL), lambda i: (i, 0)),
        out_shape=jax.ShapeDtypeStruct((m, D_MODEL), F32),
        scratch_shapes=[pltpu.VMEM((NBLK, HALO + CV_TM, LANES), F32),
                        pltpu.VMEM((NBLK, CV_TM, LANES), F32)],
        compiler_params=pltpu.CompilerParams(
            dimension_semantics=("arbitrary",), vmem_limit_bytes=VMEM_LIMIT),
        name="conv",
    )(x2, sc, sh, gt, w_pw1, wdw, bdw, row(g_norm), row(b_norm), w_pw2, row(g), row(b))


def kernel(x, c, ada_w, ada_b, ln_g, ln_b, s5_lam_re, s5_lam_im, s5_log_dt, s5_b_re, s5_b_im,
           s5_c_re, s5_c_im, s5_d, s5_w_glu, cv_w_pw1, cv_w_dw, cv_b_dw, cv_norm_g, cv_norm_b,
           cv_w_pw2, ffn_w_gu, ffn_w_down):
    mod = _ada(c, ada_w, ada_b)
    mods = [[mod[i, :, None, k * D_MODEL:(k + 1) * D_MODEL] for k in range(N_MOD)]
            for i in range(DEPTH)]

    sh_m, sc_m, gt_m, sh_f, sc_f, gt_f = mods[0]
    s5_ops = _s5_prep(s5_lam_re[0], s5_lam_im[0], s5_log_dt[0], s5_b_re[0], s5_b_im[0],
                                 s5_c_re[0], s5_c_im[0])
    xr = x.reshape(BATCH, ROWS, TAU * D_MODEL)
    ycat = _s5(xr, sc_m, sh_m, s5_d[0], *s5_ops)
    x1 = _glu_ln(ycat, xr, gt_m, s5_w_glu[0].astype(BF16), ln_g[0, 0], ln_b[0, 0])
    x2 = x1.reshape(BATCH * SEQ, D_MODEL)
    x2 = _ffn(x2, sc_f, sh_f, gt_f, ffn_w_gu[0].astype(BF16), ffn_w_down[0].astype(BF16),
              ln_g[0, 1], ln_b[0, 1])

    sh_m, sc_m, gt_m, sh_f, sc_f, gt_f = mods[1]
    x2 = _conv(x2, sc_m, sh_m, gt_m, cv_w_pw1[0].astype(BF16), cv_w_dw[0], cv_b_dw[0],
               cv_norm_g[0], cv_norm_b[0], cv_w_pw2[0].astype(BF16), ln_g[1, 0], ln_b[1, 0])
    x2 = _ffn(x2, sc_f, sh_f, gt_f, ffn_w_gu[1].astype(BF16), ffn_w_down[1].astype(BF16),
              ln_g[1, 1], ln_b[1, 1])
    return x2.reshape(BATCH, SEQ, D_MODEL)
```

```python
import functools
import math

import jax
import jax.numpy as jnp
from jax import lax
from jax.experimental import pallas as pl
from jax.experimental.pallas import tpu as pltpu

F32 = jnp.float32
BF16 = jnp.bfloat16

D_MODEL = 2048
BATCH = 4
SEQ = 8192
DEPTH = 2
S5_GROUP = 16
S5_GROUPS = D_MODEL // S5_GROUP
S5_STATE = 64
CONV_WIDTH = 31
D_FF = 5632
ALPHA = (2 * DEPTH) ** 0.25
LN_EPS = 1e-5
N_MOD = 6

LANES = 128
SUBLANES = 8
MXU_DIM = 256

TAU = 8
ROWS = SEQ // TAU
GPB = LANES // S5_GROUP
NBLK = D_MODEL // LANES
NPAIR = D_MODEL // MXU_DIM
KDIM = TAU * LANES
SDIM = GPB * S5_STATE * 2
S5_RT = 512
NSEG = SUBLANES
SEG = 66
S5_RP = NSEG * SEG
NSLAB = SDIM // LANES

VMEM_LIMIT = 56 * 1024 * 1024


def _layer_norm(z, g, b):
    mu = jnp.mean(z, axis=-1, keepdims=True)
    zc = z - mu
    var = jnp.mean(zc * zc, axis=-1, keepdims=True)
    return zc * lax.rsqrt(var + LN_EPS) * g + b


def _gelu_tanh(x):
    c = math.sqrt(2.0 / math.pi)
    return x * (0.5 * (1.0 + jnp.tanh(c * (x + 0.044715 * (x * x * x)))))


def _bdot(a, b):
    return jnp.dot(a, b, preferred_element_type=F32)


ADA_TN = 1024


def _ada_kernel(c_ref, w_ref, b_ref, o_ref):
    c = c_ref[...]
    ca = c * jax.nn.sigmoid(c)
    o_ref[0] = jnp.dot(ca, w_ref[0], preferred_element_type=F32,
                       precision=lax.Precision.HIGHEST) + b_ref[0]


def _ada(c, ada_w, ada_b):
    cp = jnp.zeros((SUBLANES, D_MODEL), F32).at[:BATCH].set(c)
    n = N_MOD * D_MODEL
    out = pl.pallas_call(
        _ada_kernel,
        grid=(DEPTH, n // ADA_TN),
        in_specs=[
            pl.BlockSpec((SUBLANES, D_MODEL), lambda i, j: (0, 0)),
            pl.BlockSpec((1, D_MODEL, ADA_TN), lambda i, j: (i, 0, j)),
            pl.BlockSpec((1, 1, ADA_TN), lambda i, j: (i, 0, j)),
        ],
        out_specs=pl.BlockSpec((1, SUBLANES, ADA_TN), lambda i, j: (i, 0, j)),
        out_shape=jax.ShapeDtypeStruct((DEPTH, SUBLANES, n), F32),
        compiler_params=pltpu.CompilerParams(
            dimension_semantics=("parallel", "parallel"), vmem_limit_bytes=VMEM_LIMIT),
        name="ada",
    )(cp, ada_w, ada_b.reshape(DEPTH, 1, n))
    return out[:, :BATCH]


def _cmul(a, b):
    return a[0] * b[0] - a[1] * b[1], a[0] * b[1] + a[1] * b[0]


def _s5_prep(lam_re, lam_im, log_dt, b_re, b_im, c_re, c_im):
    hp = lax.Precision.HIGHEST
    dt = jnp.exp(log_dt)[:, None]
    zr, zi = lam_re * dt, lam_im * dt
    er = jnp.exp(zr)
    a = (er * jnp.cos(zi), er * jnp.sin(zi))
    sh = jnp.sin(0.5 * zi)
    nr, ni = jnp.expm1(zr) * jnp.cos(zi) - 2.0 * sh * sh, a[1]
    den = lam_re * lam_re + lam_im * lam_im
    q = ((nr * lam_re + ni * lam_im) / den, (ni * lam_re - nr * lam_im) / den)
    bb = (q[0][..., None] * b_re - q[1][..., None] * b_im,
          q[0][..., None] * b_im + q[1][..., None] * b_re)
    pows = [(jnp.ones_like(zr), jnp.zeros_like(zr))]
    for _ in range(TAU):
        pows.append(_cmul(pows[-1], a))
    pr = jnp.stack([p[0] for p in pows])
    pi = jnp.stack([p[1] for p in pows])
    car = c_re[None] * pr[:, :, None, :] - c_im[None] * pi[:, :, None, :]
    cai = c_re[None] * pi[:, :, None, :] + c_im[None] * pr[:, :, None, :]
    kk = (jnp.einsum("kgop,gpi->kgoi", car[:TAU], bb[0], precision=hp)
          - jnp.einsum("kgop,gpi->kgoi", cai[:TAU], bb[1], precision=hp))
    g2 = (NBLK, GPB)

    def spread(compact, width):
        rep = jnp.tile(jnp.eye(width, dtype=BF16), (1, GPB))
        return jnp.einsum("...w,wy->...y", compact.astype(BF16), rep,
                          preferred_element_type=F32).astype(BF16)

    def group_mask(rows_per, cols_per):
        r = jnp.arange(GPB * rows_per)[:, None] // rows_per
        c = jnp.arange(GPB * cols_per)[None, :] // cols_per
        return (r == c).astype(BF16)

    kc = jnp.transpose(kk.reshape(TAU, *g2, S5_GROUP, S5_GROUP), (0, 1, 2, 4, 3))
    e = spread(kc.reshape(TAU, NBLK, LANES, S5_GROUP), S5_GROUP) * group_mask(S5_GROUP, S5_GROUP)
    zero = jnp.zeros((NBLK, LANES, LANES), BF16)
    w_toep = jnp.concatenate(
        [jnp.concatenate([e[jo - j] if jo >= j else zero for jo in range(TAU)], axis=2)
         for j in range(TAU)], axis=1)

    vr = pr[TAU - 1::-1][:, :, :, None] * bb[0][None] - pi[TAU - 1::-1][:, :, :, None] * bb[1][None]
    vi = pr[TAU - 1::-1][:, :, :, None] * bb[1][None] + pi[TAU - 1::-1][:, :, :, None] * bb[0][None]
    in_mask = jnp.tile(group_mask(S5_GROUP, S5_STATE), (TAU, 1))

    def in_part(vp):
        vp = jnp.transpose(vp.reshape(TAU, *g2, S5_STATE, S5_GROUP), (1, 0, 2, 4, 3))
        return spread(vp.reshape(NBLK, KDIM, S5_STATE), S5_STATE) * in_mask

    w_in = jnp.concatenate([in_part(vr), in_part(vi)], axis=2)

    co = jnp.stack([car[1:], -cai[1:]], axis=0)
    co = co.reshape(2, TAU, *g2, S5_GROUP, S5_STATE)
    co = jnp.transpose(co, (1, 2, 0, 3, 5, 4))
    out_mask = jnp.tile(group_mask(S5_STATE, S5_GROUP), (2, 1))
    wo = spread(co.reshape(TAU, NBLK, SDIM, S5_GROUP), S5_GROUP) * out_mask
    w_out = jnp.concatenate([wo[jo] for jo in range(TAU)], axis=2)

    a_t = pows[TAU]
    sq = [a_t]
    for _ in range(6):
        sq.append(_cmul(sq[-1], sq[-1]))
    assert SEG == 66
    a_seg = _cmul(sq[6], sq[1])
    coef = jnp.stack([a_t[0], a_t[1], a_seg[0], a_seg[1]], axis=0)
    coef = jnp.transpose(coef.reshape(4, NBLK, GPB * S5_STATE), (1, 0, 2))

    return w_toep, w_in, w_out, coef


def _s5_kernel(*refs):
    x_refs = refs[:2]
    (sc_ref, sh_ref, d_ref, wtoep_ref, win_ref, wout_ref, coef_ref, o_ref,
     x_all, hp_all, carry_scr, s_all) = refs[2:]
    half = NSLAB // 2

    @pl.when(pl.program_id(2) == 0)
    def _():
        carry_scr[...] = jnp.zeros_like(carry_scr)

    for hf in range(2):
        x_scr, hp_scr, s_scr = x_all.at[hf], hp_all.at[hf], s_all.at[hf]
        ls = hf * LANES
        scale = 1.0 + sc_ref[0, :, ls:ls + LANES]
        shift = sh_ref[0, :, ls:ls + LANES]

        def h_of(j):
            return x_refs[hf][0, pl.ds(j, S5_RT, stride=TAU), :] * scale + shift

        u = jnp.concatenate([h_of(j).astype(BF16) for j in range(TAU)], axis=1)
        xin = _bdot(u, win_ref[hf])
        for l in range(NSLAB):
            x_scr[l, 0:S5_RT, :] = xin[:, l * LANES:(l + 1) * LANES]
            x_scr[l, S5_RT:S5_RP, :] = jnp.zeros((S5_RP - S5_RT, LANES), F32)

        def coef(row, l):
            return jnp.broadcast_to(coef_ref[hf, row:row + 1, l * LANES:(l + 1) * LANES],
                                    (NSEG, LANES))

        ar = [coef(0, l) for l in range(half)]
        ai = [coef(1, l) for l in range(half)]

        def step(k, hr, hi):
            nr, ni = [], []
            for l in range(half):
                xr = x_scr[l, pl.ds(k, NSEG, stride=SEG), :]
                xi = x_scr[l + half, pl.ds(k, NSEG, stride=SEG), :]
                nr.append(ar[l] * hr[l] - ai[l] * hi[l] + xr)
                ni.append(ar[l] * hi[l] + ai[l] * hr[l] + xi)
            return tuple(nr), tuple(ni)

        zero = tuple(jnp.zeros((NSEG, LANES), F32) for _ in range(half))
        er, ei = zero, zero
        for k in range(SEG):
            er, ei = step(k, er, ei)

        cr = [carry_scr[hf * NSLAB + l, 0:1, :] for l in range(half)]
        ci = [carry_scr[hf * NSLAB + half + l, 0:1, :] for l in range(half)]
        for s in range(NSEG):
            for l in range(half):
                s_scr[l, s:s + 1, :] = cr[l]
                s_scr[l + half, s:s + 1, :] = ci[l]
            if s + 1 < NSEG:
                for l in range(half):
                    gr = coef_ref[hf, 2:3, l * LANES:(l + 1) * LANES]
                    gi = coef_ref[hf, 3:4, l * LANES:(l + 1) * LANES]
                    nr = gr * cr[l] - gi * ci[l] + er[l][s:s + 1, :]
                    ni = gr * ci[l] + gi * cr[l] + ei[l][s:s + 1, :]
                    cr[l], ci[l] = nr, ni

        hr = tuple(s_scr[l] for l in range(half))
        hi = tuple(s_scr[l + half] for l in range(half))
        for k in range(SEG):
            for l in range(half):
                hp_scr[l, pl.ds(k, NSEG, stride=SEG), :] = hr[l]
                hp_scr[l + half, pl.ds(k, NSEG, stride=SEG), :] = hi[l]
            hr, hi = step(k, hr, hi)
        for l in range(NSLAB):
            carry_scr[hf * NSLAB + l, 0:1, :] = hp_scr[l, S5_RT:S5_RT + 1, :]

        hp = jnp.concatenate([hp_scr[l, 0:S5_RT, :].astype(BF16) for l in range(NSLAB)], axis=1)
        dsk = d_ref[:, ls:ls + LANES]
        for q in range(KDIM // MXU_DIM):
            c0, kq = q * MXU_DIM, (q + 1) * MXU_DIM
            y = (_bdot(u[:, :kq], wtoep_ref[hf, :kq, c0:c0 + MXU_DIM])
                 + _bdot(hp, wout_ref[hf, :, c0:c0 + MXU_DIM]))
            for jj in range(MXU_DIM // LANES):
                j = q * (MXU_DIM // LANES) + jj
                yj = y[:, jj * LANES:(jj + 1) * LANES] + dsk * h_of(j)
                o_ref[0, hf, pl.ds(j, S5_RT, stride=TAU), :] = _gelu_tanh(yj)


def _s5(x, sc, sh, d_skip, w_toep, w_in, w_out, coef):
    x_specs = [
        pl.BlockSpec((1, TAU * S5_RT, LANES), functools.partial(
            lambda bp, b, rt, hf: (b, rt, 2 * bp + hf), hf=hf))
        for hf in range(2)
    ]
    mod_spec = pl.BlockSpec((1, 1, MXU_DIM), lambda bp, b, rt: (b, 0, bp))
    return pl.pallas_call(
        _s5_kernel,
        grid=(NPAIR, BATCH, ROWS // S5_RT),
        in_specs=x_specs + [
            mod_spec, mod_spec,
            pl.BlockSpec((1, MXU_DIM), lambda bp, b, rt: (0, bp)),
            pl.BlockSpec((2, KDIM, KDIM), lambda bp, b, rt: (bp, 0, 0),
                         pipeline_mode=pl.Buffered(1)),
            pl.BlockSpec((2, KDIM, SDIM), lambda bp, b, rt: (bp, 0, 0),
                         pipeline_mode=pl.Buffered(1)),
            pl.BlockSpec((2, SDIM, KDIM), lambda bp, b, rt: (bp, 0, 0),
                         pipeline_mode=pl.Buffered(1)),
            pl.BlockSpec((2, 4, SDIM // 2), lambda bp, b, rt: (bp, 0, 0)),
        ],
        out_specs=pl.BlockSpec((1, 2, TAU * S5_RT, LANES), lambda bp, b, rt: (b, bp, rt, 0)),
        out_shape=jax.ShapeDtypeStruct((BATCH, NBLK, SEQ, LANES), F32),
        scratch_shapes=[
            pltpu.VMEM((2, NSLAB, S5_RP, LANES), F32),
            pltpu.VMEM((2, NSLAB, S5_RP, LANES), F32),
            pltpu.VMEM((2 * NSLAB, SUBLANES, LANES), F32),
            pltpu.VMEM((2, NSLAB, NSEG, LANES), F32),
        ],
        compiler_params=pltpu.CompilerParams(
            dimension_semantics=("parallel", "parallel", "arbitrary"),
            vmem_limit_bytes=VMEM_LIMIT),
        name="s5",
    )(x, x, sc, sh, d_skip.reshape(1, D_MODEL), w_toep, w_in, w_out, coef)


GLU_RT = 512
COL_T = 512


def _glu_ln_kernel(y_ref, x_ref, gt_ref, w_ref, g_ref, b_ref, o_ref):
    y = jnp.concatenate([y_ref[0, l].astype(BF16) for l in range(NBLK)], axis=1)
    zs = []
    for ct in range(D_MODEL // COL_T):
        c0 = ct * COL_T
        v = _bdot(y, w_ref[:, c0:c0 + COL_T])
        g = _bdot(y, w_ref[:, D_MODEL + c0:D_MODEL + c0 + COL_T])
        gate = 1.0 + gt_ref[0, :, c0:c0 + COL_T]
        zs.append(ALPHA * x_ref[0, :, c0:c0 + COL_T] + gate * (v * jax.nn.sigmoid(g)))
    z = jnp.concatenate(zs, axis=1)
    o_ref[0] = _layer_norm(z, g_ref[...], b_ref[...])


def _glu_ln(ycat, x, gt, w_glu, g, b):
    vec = pl.BlockSpec((1, D_MODEL), lambda bb, t: (0, 0))
    return pl.pallas_call(
        _glu_ln_kernel,
        grid=(BATCH, SEQ // GLU_RT),
        in_specs=[
            pl.BlockSpec((1, NBLK, GLU_RT, LANES), lambda bb, t: (bb, 0, t, 0)),
            pl.BlockSpec((1, GLU_RT, D_MODEL), lambda bb, t: (bb, t, 0)),
            pl.BlockSpec((1, 1, D_MODEL), lambda bb, t: (bb, 0, 0)),
            pl.BlockSpec((D_MODEL, 2 * D_MODEL), lambda bb, t: (0, 0),
                         pipeline_mode=pl.Buffered(1)),
            vec, vec,
        ],
        out_specs=pl.BlockSpec((1, GLU_RT, D_MODEL), lambda bb, t: (bb, t, 0)),
        out_shape=jax.ShapeDtypeStruct((BATCH, SEQ, D_MODEL), F32),
        compiler_params=pltpu.CompilerParams(
            dimension_semantics=("parallel", "parallel"),
            vmem_limit_bytes=VMEM_LIMIT),
        name="glu_ln",
    )(ycat, x, gt, w_glu, g.reshape(1, D_MODEL), b.reshape(1, D_MODEL))


FFN_TM = 512
FFN_TF = 512


def _ffn_kernel(x_ref, sc_ref, sh_ref, gt_ref, wg_ref, wv_ref, wd_ref, g_ref, b_ref, o_ref,
                h_scr, acc_scr):
    f = pl.program_id(1)

    @pl.when(f == 0)
    def _():
        h_scr[...] = (x_ref[...] * (1.0 + sc_ref[0]) + sh_ref[0]).astype(BF16)
        acc_scr[...] = jnp.zeros_like(acc_scr)

    h = h_scr[...]
    g = _bdot(h, wg_ref[...])
    v = _bdot(h, wv_ref[...])
    a = (g * jax.nn.sigmoid(g) * v).astype(BF16)
    acc_scr[...] += _bdot(a, wd_ref[...])

    @pl.when(f == pl.num_programs(1) - 1)
    def _():
        z = ALPHA * x_ref[...] + (1.0 + gt_ref[0]) * acc_scr[...]
        o_ref[...] = _layer_norm(z, g_ref[...], b_ref[...])


def _ffn(x2, sc, sh, gt, w_gu, w_down, g, b):
    m = x2.shape[0]
    tpb = SEQ // FFN_TM
    nf = D_FF // FFN_TF
    mod = pl.BlockSpec((1, 1, D_MODEL), lambda i, f: (i // tpb, 0, 0))
    vec = pl.BlockSpec((1, D_MODEL), lambda i, f: (0, 0))
    return pl.pallas_call(
        _ffn_kernel,
        grid=(m // FFN_TM, nf),
        in_specs=[
            pl.BlockSpec((FFN_TM, D_MODEL), lambda i, f: (i, 0)),
            mod, mod, mod,
            pl.BlockSpec((D_MODEL, FFN_TF), lambda i, f: (0, f)),
            pl.BlockSpec((D_MODEL, FFN_TF), lambda i, f: (0, nf + f)),
            pl.BlockSpec((FFN_TF, D_MODEL), lambda i, f: (f, 0)),
            vec, vec,
        ],
        out_specs=pl.BlockSpec((FFN_TM, D_MODEL), lambda i, f: (i, 0)),
        out_shape=jax.ShapeDtypeStruct((m, D_MODEL), F32),
        scratch_shapes=[pltpu.VMEM((FFN_TM, D_MODEL), BF16), pltpu.VMEM((FFN_TM, D_MODEL), F32)],
        compiler_params=pltpu.CompilerParams(
            dimension_semantics=("parallel", "arbitrary"), vmem_limit_bytes=VMEM_LIMIT),
        name="ffn",
    )(x2, sc, sh, gt, w_gu, w_gu, w_down, g.reshape(1, D_MODEL), b.reshape(1, D_MODEL))


CV_TM = 256
CV_STRIDE = 4
CV_BAND = SUBLANES * CV_STRIDE
HALO = 32


def _conv_kernel(x_ref, sc_ref, sh_ref, gt_ref, w1_ref, wdw_ref, bdw_ref, gn_ref, bn_ref, w2_ref,
                 g_ref, b_ref, o_ref, vbuf, cbuf):
    tpb = SEQ // CV_TM

    @pl.when(pl.program_id(0) % tpb == 0)
    def _():
        vbuf[:, 0:HALO, :] = jnp.zeros((NBLK, HALO, LANES), F32)

    def conv_slab(l):
        w = [jnp.broadcast_to(wdw_ref[l, k:k + 1, :], (SUBLANES, LANES)) for k in range(CONV_WIDTH)]
        bias = jnp.broadcast_to(bdw_ref[l], (SUBLANES, LANES))
        for band in range(CV_TM // CV_BAND):
            base = band * CV_BAND
            acc = [bias] * CV_STRIDE
            for r in range(2, CONV_WIDTH + 1 + CV_STRIDE):
                tap = vbuf[l, pl.ds(base + r, SUBLANES, stride=CV_STRIDE), :]
                for t0 in range(CV_STRIDE):
                    k = r - t0 - 2
                    if 0 <= k < CONV_WIDTH:
                        acc[t0] = acc[t0] + w[k] * tap
            for t0 in range(CV_STRIDE):
                cbuf[l, pl.ds(base + t0, SUBLANES, stride=CV_STRIDE), :] = acc[t0]
        vbuf[l, 0:HALO, :] = vbuf[l, CV_TM:CV_TM + HALO, :]

    h = (x_ref[...] * (1.0 + sc_ref[0]) + sh_ref[0]).astype(BF16)
    for ct in range(D_MODEL // COL_T):
        c0 = ct * COL_T
        a = _bdot(h, w1_ref[:, c0:c0 + COL_T])
        gl = _bdot(h, w1_ref[:, D_MODEL + c0:D_MODEL + c0 + COL_T])
        v = a * jax.nn.sigmoid(gl)
        for s in range(COL_T // LANES):
            l = ct * (COL_T // LANES) + s
            vbuf[l, HALO:, :] = v[:, s * LANES:(s + 1) * LANES]
            conv_slab(l)

    conv = jnp.concatenate([cbuf[l] for l in range(NBLK)], axis=1)
    u = _layer_norm(conv, gn_ref[...], bn_ref[...])
    u = (u * jax.nn.sigmoid(u)).astype(BF16)
    y = _bdot(u, w2_ref[...])
    z = ALPHA * x_ref[...] + (1.0 + gt_ref[0]) * y
    o_ref[...] = _layer_norm(z, g_ref[...], b_ref[...])


def _conv(x2, sc, sh, gt, w_pw1, w_dw, b_dw, g_norm, b_norm, w_pw2, g, b):
    m = x2.shape[0]
    tpb = SEQ // CV_TM
    mod = pl.BlockSpec((1, 1, D_MODEL), lambda i: (i // tpb, 0, 0))
    vec = pl.BlockSpec((1, D_MODEL), lambda i: (0, 0))
    wdw = jnp.zeros((HALO, D_MODEL), F32).at[:CONV_WIDTH].set(w_dw)
    wdw = jnp.transpose(wdw.reshape(HALO, NBLK, LANES), (1, 0, 2))
    bdw = b_dw.reshape(NBLK, 1, LANES)
    row = lambda a: a.reshape(1, D_MODEL)
    return pl.pallas_call(
        _conv_kernel,
        grid=(m // CV_TM,),
        in_specs=[
            pl.BlockSpec((CV_TM, D_MODEL), lambda i: (i, 0)),
            mod, mod, mod,
            pl.BlockSpec((D_MODEL, 2 * D_MODEL), lambda i: (0, 0), pipeline_mode=pl.Buffered(1)),
            pl.BlockSpec((NBLK, HALO, LANES), lambda i: (0, 0, 0)),
            pl.BlockSpec((NBLK, 1, LANES), lambda i: (0, 0, 0)),
            vec, vec,
            pl.BlockSpec((D_MODEL, D_MODEL), lambda i: (0, 0), pipeline_mode=pl.Buffered(1)),
            vec, vec,
        ],
        out_specs=pl.BlockSpec((CV_TM, D_MODEL), lambda i: (i, 0)),
        out_shape=jax.ShapeDtypeStruct((m, D_MODEL), F32),
        scratch_shapes=[pltpu.VMEM((NBLK, HALO + CV_TM, LANES), F32),
                        pltpu.VMEM((NBLK, CV_TM, LANES), F32)],
        compiler_params=pltpu.CompilerParams(
            dimension_semantics=("arbitrary",), vmem_limit_bytes=VMEM_LIMIT),
        name="conv",
    )(x2, sc, sh, gt, w_pw1, wdw, bdw, row(g_norm), row(b_norm), w_pw2, row(g), row(b))


def kernel(x, c, ada_w, ada_b, ln_g, ln_b, s5_lam_re, s5_lam_im, s5_log_dt, s5_b_re, s5_b_im,
           s5_c_re, s5_c_im, s5_d, s5_w_glu, cv_w_pw1, cv_w_dw, cv_b_dw, cv_norm_g, cv_norm_b,
           cv_w_pw2, ffn_w_gu, ffn_w_down):
    mod = _ada(c, ada_w, ada_b)
    mods = [[mod[i, :, None, k * D_MODEL:(k + 1) * D_MODEL] for k in range(N_MOD)]
            for i in range(DEPTH)]

    sh_m, sc_m, gt_m, sh_f, sc_f, gt_f = mods[0]
    s5_ops = _s5_prep(s5_lam_re[0], s5_lam_im[0], s5_log_dt[0], s5_b_re[0], s5_b_im[0],
                                 s5_c_re[0], s5_c_im[0])
    ycat = _s5(x, sc_m, sh_m, s5_d[0], *s5_ops)
    x1 = _glu_ln(ycat, x, gt_m, s5_w_glu[0].astype(BF16), ln_g[0, 0], ln_b[0, 0])
    x2 = x1.reshape(BATCH * SEQ, D_MODEL)
    x2 = _ffn(x2, sc_f, sh_f, gt_f, ffn_w_gu[0].astype(BF16), ffn_w_down[0].astype(BF16),
              ln_g[0, 1], ln_b[0, 1])

    sh_m, sc_m, gt_m, sh_f, sc_f, gt_f = mods[1]
    x2 = _conv(x2, sc_m, sh_m, gt_m, cv_w_pw1[0].astype(BF16), cv_w_dw[0], cv_b_dw[0],
               cv_norm_g[0], cv_norm_b[0], cv_w_pw2[0].astype(BF16), ln_g[1, 0], ln_b[1, 0])
    x2 = _ffn(x2, sc_f, sh_f, gt_f, ffn_w_gu[1].astype(BF16), ffn_w_down[1].astype(BF16),
              ln_g[1, 1], ln_b[1, 1])
    return x2.reshape(BATCH, SEQ, D_MODEL)
```
